```python
import jax, jax.numpy as jnp
from jax import lax
import numpy as np

D_MODEL = 4096
BATCH = 2
SEQ = 8192
DEPTH = 2

CHUNK = 64
EPS = 1e-6
HG_DK = 128
HG_HEADS = D_MODEL // (2 * HG_DK)
HG_DV = 128
HG_WIDTH = HG_HEADS * HG_DV
HG_FDIM = HG_HEADS * HG_DK
MLA_DV = 128
MLA_HEADS = D_MODEL // (2 * MLA_DV)
MLA_NOPE = 128
MLA_ROPE = 64
MLA_Q_RANK = 768
MLA_KV_RANK = 512
MLA_WIDTH = MLA_HEADS * MLA_DV
ROPE_THETA = 10000.0
Q_BLOCK = 128
IN_SIZES = (HG_FDIM, HG_FDIM, HG_WIDTH, HG_WIDTH, MLA_Q_RANK, MLA_KV_RANK, MLA_ROPE, D_MODEL, D_MODEL)
IN_COLS = 2 * HG_FDIM + 2 * HG_WIDTH + MLA_Q_RANK + MLA_KV_RANK + MLA_ROPE + 2 * D_MODEL
D_FF = ((8 * D_MODEL // 3 + 255) // 256) * 256
N_EXPERTS = 8
TOP_K = 2
D_FF_EXPERT = D_MODEL
N_DENSE = (DEPTH + 1) // 2
N_MOE = DEPTH // 2

kernel_name = "hgrn2_mla_gated_hybrid_moe"


def rmsnorm(x, g):
    xf = x.astype(jnp.float32)
    y = xf * lax.rsqrt(jnp.mean(xf * xf, axis=-1, keepdims=True) + EPS)
    return (y * g.astype(jnp.float32)).astype(x.dtype)


def apply_rope(x, cos, sin):
    half = x.shape[-1] // 2
    xf = x.astype(jnp.float32)
    x1, x2 = xf[..., :half], xf[..., half:]
    return jnp.concatenate([x1 * cos - x2 * sin, x2 * cos + x1 * sin], axis=-1).astype(x.dtype)


def hgrn2_branch(q_raw, f_raw, i_raw, og_raw, lb, gain):
    B, S, _ = q_raw.shape
    NC = S // CHUNK
    z = f_raw.astype(jnp.float32)
    log_f = jnp.logaddexp(jnp.log(lb), jnp.log1p(-lb) + jax.nn.log_sigmoid(z))
    k = -jnp.expm1(log_f)
    q = jax.nn.silu(q_raw.astype(jnp.float32))
    v = i_raw.astype(jnp.float32)

    def to_chunks(t, d):
        return t.reshape(B, NC, CHUNK, HG_HEADS, d).transpose(1, 0, 3, 2, 4)

    causal = jnp.tril(jnp.ones((CHUNK, CHUNK), dtype=bool))

    def step(state, xs):
        qc, kc, vc, gc = xs
        b = jnp.cumsum(gc, axis=2)
        b_last = b[:, :, -1, :]
        o_inter = jnp.einsum('bhtc,bhcv->bhtv', qc * jnp.exp(b), state)
        diff = b[:, :, :, None, :] - b[:, :, None, :, :]
        decay = jnp.where(causal[:, :, None], jnp.exp(jnp.minimum(diff, 0.0)), 0.0)
        scores = jnp.einsum('bhtc,bhsc,bhtsc->bhts', qc, kc, decay)
        o_intra = jnp.einsum('bhts,bhsv->bhtv', scores, vc)
        k_dec = kc * jnp.exp(b_last[:, :, None, :] - b)
        state = jnp.exp(b_last)[..., None] * state + jnp.einsum('bhsc,bhsv->bhcv', k_dec, vc)
        return state, o_intra + o_inter

    state0 = jnp.zeros((B, HG_HEADS, HG_DK, HG_DV), jnp.float32)
    _, o = lax.scan(step, state0, (to_chunks(q, HG_DK), to_chunks(k, HG_DK), to_chunks(v, HG_DV), to_chunks(log_f, HG_DK)))
    o = o.transpose(1, 0, 3, 2, 4).reshape(B, S, HG_HEADS, HG_DV)
    o = o * lax.rsqrt(jnp.mean(o * o, axis=-1, keepdims=True) + EPS)
    o = o.reshape(B, S, HG_WIDTH) * gain.astype(jnp.float32) * jax.nn.silu(og_raw.astype(jnp.float32))
    return o.astype(q_raw.dtype)


def mla_branch(c_q, c_kv, k_pe_raw, q_gain, kv_gain, w_uq, w_ukv, cos, sin):
    B, S, _ = c_q.shape
    q = (rmsnorm(c_q, q_gain) @ w_uq).reshape(B, S, MLA_HEADS, MLA_NOPE + MLA_ROPE)
    q_nope = q[..., :MLA_NOPE]
    q_pe = apply_rope(q[..., MLA_NOPE:], cos[:, :, None, :], sin[:, :, None, :])
    kv = (rmsnorm(c_kv, kv_gain) @ w_ukv).reshape(B, S, MLA_HEADS, MLA_NOPE + MLA_DV)
    k_nope, v = kv[..., :MLA_NOPE], kv[..., MLA_NOPE:]
    k_pe = apply_rope(k_pe_raw, cos, sin)
    scale = (MLA_NOPE + MLA_ROPE) ** -0.5
    key_chunk = jnp.arange(S) // CHUNK

    def block(j):
        start = j * Q_BLOCK
        qn = lax.dynamic_slice_in_dim(q_nope, start, Q_BLOCK, axis=1)
        qp = lax.dynamic_slice_in_dim(q_pe, start, Q_BLOCK, axis=1)
        s = (jnp.einsum('bqhd,bkhd->bhqk', qn, k_nope) + jnp.einsum('bqhr,bkr->bhqk', qp, k_pe)).astype(jnp.float32) * scale
        q_chunk = (start + jnp.arange(Q_BLOCK)) // CHUNK
        mask = key_chunk[None, :] <= q_chunk[:, None]
        p = jax.nn.softmax(jnp.where(mask, s, -jnp.inf), axis=-1).astype(v.dtype)
        return jnp.einsum('bhqk,bkhd->bqhd', p, v)

    o = lax.map(block, jnp.arange(S // Q_BLOCK))
    return o.transpose(1, 0, 2, 3, 4).reshape(B, S, MLA_WIDTH)


def hybrid_mixer(h, w_in, lb, hg_gain, q_gain, kv_gain, w_uq, w_ukv, w_branch, w_out, cos, sin):
    proj = h @ w_in
    points = np.cumsum(np.array(IN_SIZES))[:-1].tolist()
    hq, hf, hi, hog, cq, ckv, kpe, ga, gb = jnp.split(proj, points, axis=-1)
    o_a = hgrn2_branch(hq, hf, hi, hog, lb, hg_gain)
    o_b = mla_branch(cq, ckv, kpe, q_gain, kv_gain, w_uq, w_ukv, cos, sin)
    y_a = o_a @ w_branch[:HG_WIDTH]
    y_b = o_b @ w_branch[HG_WIDTH:]
    y = jax.nn.sigmoid(ga) * y_a + jax.nn.sigmoid(gb) * y_b
    return y @ w_out


def swiglu(h, w1, w3, w2):
    return (jax.nn.silu(h @ w1) * (h @ w3)) @ w2


def moe_ffn(h, w_router, w1, w3, w2):
    B, S, D = h.shape
    t = h.reshape(B * S, D)
    logits = (t @ w_router).astype(jnp.float32)
    top_v, top_i = lax.top_k(logits, TOP_K)
    top_w = jax.nn.softmax(top_v, axis=-1)
    combine = jnp.sum(jax.nn.one_hot(top_i, N_EXPERTS, dtype=jnp.float32) * top_w[..., None], axis=1)
    out = jnp.zeros_like(t)
    for e in range(N_EXPERTS):
        out = out + combine[:, e:e + 1].astype(t.dtype) * swiglu(t, w1[e], w3[e], w2[e])
    return out.reshape(B, S, D)


def setup_inputs(seed: int = 0) -> dict:
    key = jax.random.key(seed)
    ks = jax.random.split(key, 24)

    def nrm(k, shape, fan_in):
        return jax.random.normal(k, shape, jnp.float32) * fan_in ** -0.5

    def gain(k, shape):
        return 1.0 + 0.02 * jax.random.normal(k, shape, jnp.float32)

    x = jax.random.normal(ks[0], (BATCH, SEQ, D_MODEL), jnp.float32)
    offsets = jax.random.randint(ks[1], (BATCH, 1), 0, 4096, dtype=jnp.int32)
    positions = offsets + jnp.arange(SEQ, dtype=jnp.int32)[None, :]
    return {
        "x": x,
        "positions": positions,
        "norm_mix": gain(ks[2], (DEPTH, D_MODEL)),
        "w_in": nrm(ks[3], (DEPTH, D_MODEL, IN_COLS), D_MODEL),
        "hg_lb_logits": 0.5 * jax.random.normal(ks[4], (DEPTH, HG_FDIM), jnp.float32),
        "hg_norm": gain(ks[5], (DEPTH, HG_WIDTH)),
        "mla_q_norm": gain(ks[6], (DEPTH, MLA_Q_RANK)),
        "w_uq": nrm(ks[7], (DEPTH, MLA_Q_RANK, MLA_HEADS * (MLA_NOPE + MLA_ROPE)), MLA_Q_RANK),
        "mla_kv_norm": gain(ks[8], (DEPTH, MLA_KV_RANK)),
        "w_ukv": nrm(ks[9], (DEPTH, MLA_KV_RANK, MLA_HEADS * (MLA_NOPE + MLA_DV)), MLA_KV_RANK),
        "w_branch": nrm(ks[10], (DEPTH, HG_WIDTH + MLA_WIDTH, D_MODEL), HG_WIDTH),
        "w_out": nrm(ks[11], (DEPTH, D_MODEL, D_MODEL), D_MODEL),
        "norm_ffn": gain(ks[12], (DEPTH, D_MODEL)),
        "ffn_w1": nrm(ks[13], (N_DENSE, D_MODEL, D_FF), D_MODEL),
        "ffn_w3": nrm(ks[14], (N_DENSE, D_MODEL, D_FF), D_MODEL),
        "ffn_w2": nrm(ks[15], (N_DENSE, D_FF, D_MODEL), D_FF),
        "w_router": nrm(ks[16], (N_MOE, D_MODEL, N_EXPERTS), D_MODEL),
        "moe_w1": nrm(ks[17], (N_MOE, N_EXPERTS, D_MODEL, D_FF_EXPERT), D_MODEL),
        "moe_w3": nrm(ks[18], (N_MOE, N_EXPERTS, D_MODEL, D_FF_EXPERT), D_MODEL),
        "moe_w2": nrm(ks[19], (N_MOE, N_EXPERTS, D_FF_EXPERT, D_MODEL), D_FF_EXPERT),
        "norm_final": gain(ks[20], (D_MODEL,)),
    }


def reference(x, positions, norm_mix, w_in, hg_lb_logits, hg_norm, mla_q_norm, w_uq, mla_kv_norm, w_ukv,
              w_branch, w_out, norm_ffn, ffn_w1, ffn_w3, ffn_w2, w_router, moe_w1, moe_w3, moe_w2, norm_final):
    half = MLA_ROPE // 2
    inv_freq = ROPE_THETA ** (-jnp.arange(half, dtype=jnp.float32) / half)
    ang = positions.astype(jnp.float32)[..., None] * inv_freq
    cos, sin = jnp.cos(ang), jnp.sin(ang)
    lbs = jnp.cumsum(jax.nn.softmax(hg_lb_logits.astype(jnp.float32), axis=0), axis=0)
    lbs = lbs - lbs[0:1]
    for l in range(DEPTH):
        h = rmsnorm(x, norm_mix[l])
        x = x + hybrid_mixer(h, w_in[l], lbs[l], hg_norm[l], mla_q_norm[l], mla_kv_norm[l], w_uq[l], w_ukv[l],
                             w_branch[l], w_out[l], cos, sin)
        h = rmsnorm(x, norm_ffn[l])
        if l % 2 == 0:
            x = x + swiglu(h, ffn_w1[l // 2], ffn_w3[l // 2], ffn_w2[l // 2])
        else:
            x = x + moe_ffn(h, w_router[l // 2], moe_w1[l // 2], moe_w3[l // 2], moe_w2[l // 2])
    return rmsnorm(x, norm_final)
```

```python
import functools
from typing import NamedTuple

import jax
import jax.numpy as jnp
from jax import lax
from jax.experimental import pallas as pl
from jax.experimental.pallas import tpu as pltpu

F32 = jnp.float32
BF16 = jnp.bfloat16

LANES = 128
V7X_VMEM_BYTES = 64 * 1024 * 1024
VMEM_LIMIT_BYTES = V7X_VMEM_BYTES - 8 * 1024 * 1024


class Dims(NamedTuple):
    d_model: int = 4096
    batch: int = 2
    seq: int = 8192
    depth: int = 2
    chunk: int = 64
    eps: float = 1e-6
    hg_dk: int = 128
    hg_heads: int = 16
    hg_dv: int = 128
    mla_dv: int = 128
    mla_heads: int = 16
    mla_nope: int = 128
    mla_rope: int = 64
    mla_q_rank: int = 768
    mla_kv_rank: int = 512
    rope_theta: float = 10000.0
    d_ff: int = 11008
    n_experts: int = 8
    d_ff_expert: int = 4096
    tm: int = 1024
    tn: int = 1024
    tn2: int = 512
    tk_ffn: int = 2816
    d_ff_pad: int = 11264
    tm_small: int = 512
    tm_norm: int = 256
    tq: int = 256
    hgrn_rows: int = 512


SUB = 16
HG_CHUNK = 64


def _cparams(semantics):
    return pltpu.CompilerParams(dimension_semantics=semantics, vmem_limit_bytes=VMEM_LIMIT_BYTES)


def _sigmoid(x):
    return 1.0 / (1.0 + jnp.exp(-x))


def _silu(x):
    return x * _sigmoid(x)


def _rmsnorm_kernel(x_ref, g_ref, o_ref, *, eps):
    x = x_ref[...]
    ms = jnp.mean(x * x, axis=-1, keepdims=True)
    o_ref[...] = (x * lax.rsqrt(ms + eps) * g_ref[...]).astype(o_ref.dtype)


def rmsnorm(x, g, out_dtype, *, tm, eps):
    n, d = x.shape
    return pl.pallas_call(
        functools.partial(_rmsnorm_kernel, eps=eps),
        grid=(n // tm,),
        in_specs=[pl.BlockSpec((tm, d), lambda i: (i, 0)), pl.BlockSpec((1, d), lambda i: (0, 0))],
        out_specs=pl.BlockSpec((tm, d), lambda i: (i, 0)),
        out_shape=jax.ShapeDtypeStruct((n, d), out_dtype),
        compiler_params=_cparams(("parallel",)),
        name="rmsnorm",
    )(x, g.reshape(1, d))


def _mm_kernel(*refs, n_a, w_src, n_extra, n_out, nk, epilogue):
    n_w = len(w_src)
    a_refs = refs[:n_a]
    w_refs = refs[n_a:n_a + n_w]
    e_refs = refs[n_a + n_w:n_a + n_w + n_extra]
    o_refs = refs[n_a + n_w + n_extra:n_a + n_w + n_extra + n_out]
    acc_refs = refs[n_a + n_w + n_extra + n_out:]
    parts = [jnp.dot(a_refs[s][...], w[...], preferred_element_type=F32) for w, s in zip(w_refs, w_src)]
    if nk == 1:
        epilogue(parts, e_refs, o_refs)
        return
    k = pl.program_id(2)

    @pl.when(k == 0)
    def _():
        for acc, p in zip(acc_refs, parts):
            acc[...] = p

    @pl.when(k > 0)
    def _():
        for acc, p in zip(acc_refs, parts):
            acc[...] += p

    @pl.when(k == nk - 1)
    def _():
        epilogue([acc[...] for acc in acc_refs], e_refs, o_refs)


def matmul(a_list, w_list, w_src, extras, extra_specs, out_shapes, out_specs, epilogue, *, tm, tn, tk=None, name):
    m, kdim = a_list[0].shape
    n = w_list[0].shape[1]
    tk = kdim if tk is None else tk
    nk = kdim // tk
    assert m % tm == 0 and n % tn == 0 and kdim % tk == 0
    in_specs = [pl.BlockSpec((tm, tk), lambda i, j, k: (i, k)) for _ in a_list]
    in_specs += [pl.BlockSpec((tk, tn), lambda i, j, k: (k, j)) for _ in w_list]
    in_specs += list(extra_specs)
    scratch = [pltpu.VMEM((tm, tn), F32) for _ in w_list] if nk > 1 else []
    return pl.pallas_call(
        functools.partial(_mm_kernel, n_a=len(a_list), w_src=tuple(w_src), n_extra=len(extras),
                          n_out=len(out_shapes), nk=nk, epilogue=epilogue),
        grid=(m // tm, n // tn, nk),
        in_specs=in_specs,
        out_specs=list(out_specs),
        out_shape=list(out_shapes),
        scratch_shapes=scratch,
        compiler_params=_cparams(("parallel", "parallel", "arbitrary")),
        name=name,
    )(*a_list, *w_list, *extras)


def _tile_spec(tm, tn, col_block_offset=0):
    return pl.BlockSpec((tm, tn), lambda i, j, k: (i, j + col_block_offset))


def _row_spec(tm, width):
    return pl.BlockSpec((tm, width), lambda i, j, k: (i, 0))


def _ep_store(parts, e_refs, o_refs):
    o_refs[0][...] = parts[0].astype(o_refs[0].dtype)


def _ep_residual(parts, e_refs, o_refs):
    o_refs[0][...] = e_refs[0][...] + parts[0]


def _ep_swiglu(parts, e_refs, o_refs):
    o_refs[0][...] = (_silu(parts[0]) * parts[1]).astype(o_refs[0].dtype)


def _ep_gated_sum(parts, e_refs, o_refs):
    ga = e_refs[0][...].astype(F32)
    gb = e_refs[1][...].astype(F32)
    o_refs[0][...] = (_sigmoid(ga) * parts[0] + _sigmoid(gb) * parts[1]).astype(o_refs[0].dtype)


def _ep_expert_accumulate(parts, e_refs, o_refs, *, expert):
    weight = e_refs[1][...][:, expert:expert + 1]
    o_refs[0][...] = e_refs[0][...] + weight * parts[0]


def _rope128(g, c, s1, s2):
    return g * c + pltpu.roll(g, 32, 1) * s1 + pltpu.roll(g, LANES - 32, 1) * s2


def _ep_mla_pre(parts, e_refs, o_refs, *, q_rank, kv_rank, eps):
    acc = parts[0]
    qg, kvg, c, s1, s2 = (r[...] for r in e_refs)
    cq = acc[:, :q_rank]
    ckv = acc[:, q_rank:q_rank + kv_rank]
    kp = acc[:, q_rank + kv_rank:q_rank + kv_rank + LANES]
    cq = cq * lax.rsqrt(jnp.mean(cq * cq, axis=-1, keepdims=True) + eps) * qg
    ckv = ckv * lax.rsqrt(jnp.mean(ckv * ckv, axis=-1, keepdims=True) + eps) * kvg
    o_refs[0][...] = cq.astype(o_refs[0].dtype)
    o_refs[1][...] = ckv.astype(o_refs[1].dtype)
    o_refs[2][...] = _rope128(kp, c, s1, s2).astype(o_refs[2].dtype)


def _ep_mla_q(parts, e_refs, o_refs, *, heads, scale):
    acc = parts[0]
    c, s1, s2 = (r[...] for r in e_refs)
    for h in range(heads):
        base = 2 * LANES * h
        o_refs[0][h, :, :LANES] = (acc[:, base:base + LANES] * scale).astype(o_refs[0].dtype)
        pe = _rope128(acc[:, base + LANES:base + 2 * LANES], c, s1, s2)
        o_refs[0][h, :, LANES:] = (pe * scale).astype(o_refs[0].dtype)


def _ep_mla_kv(parts, e_refs, o_refs, *, heads):
    acc = parts[0]
    kpe = e_refs[0][...]
    for h in range(heads):
        base = 2 * LANES * h
        o_refs[0][h, :, :LANES] = acc[:, base:base + LANES].astype(o_refs[0].dtype)
        o_refs[0][h, :, LANES:] = kpe
        o_refs[1][h] = acc[:, base + LANES:base + 2 * LANES].astype(o_refs[1].dtype)


def _attn_kernel(q_ref, k_ref, v_ref, o_ref, *, tq, chunk):
    qi = pl.program_id(2)
    q = q_ref[...]

    def step(kb, carry, masked):
        m, l, acc = carry
        ks = pl.multiple_of(kb * tq, tq)
        k = k_ref[pl.ds(ks, tq), :]
        v = v_ref[pl.ds(ks, tq), :]
        s = lax.dot_general(q, k, (((1,), (1,)), ((), ())), preferred_element_type=F32)
        if masked:
            r = lax.broadcasted_iota(jnp.int32, (tq, tq), 0) // chunk
            c = lax.broadcasted_iota(jnp.int32, (tq, tq), 1) // chunk
            s = jnp.where(c <= r, s, -jnp.inf)
        m_new = jnp.maximum(m, jnp.max(s, axis=1, keepdims=True))
        p = jnp.exp(s - m_new)
        alpha = jnp.exp(m - m_new)
        l = alpha * l + jnp.sum(p, axis=1, keepdims=True)
        acc = alpha * acc + jnp.dot(p.astype(BF16), v, preferred_element_type=F32)
        return m_new, l, acc

    init = (jnp.full((tq, 1), -jnp.inf, F32), jnp.zeros((tq, 1), F32), jnp.zeros((tq, v_ref.shape[-1]), F32))
    carry = lax.fori_loop(0, qi, lambda kb, c: step(kb, c, False), init)
    _, l, acc = step(qi, carry, True)
    o_ref[...] = (acc / l).astype(o_ref.dtype)


def attention(q_full, k_full, v, *, tq, chunk):
    b, h, s, dq = q_full.shape
    dv = v.shape[-1]
    assert s % tq == 0 and tq % chunk == 0
    return pl.pallas_call(
        functools.partial(_attn_kernel, tq=tq, chunk=chunk),
        grid=(b, h, s // tq),
        in_specs=[
            pl.BlockSpec((None, None, tq, dq), lambda bi, hi, qi: (bi, hi, qi, 0)),
            pl.BlockSpec((None, None, s, dq), lambda bi, hi, qi: (bi, hi, 0, 0)),
            pl.BlockSpec((None, None, s, dv), lambda bi, hi, qi: (bi, hi, 0, 0)),
        ],
        out_specs=pl.BlockSpec((None, tq, dv), lambda bi, hi, qi: (bi, qi, hi)),
        out_shape=jax.ShapeDtypeStruct((b, s, h * dv), BF16),
        compiler_params=_cparams(("parallel", "parallel", "arbitrary")),
        name="mla_attention",
    )(q_full, k_full, v)


def _split3(x):
    hi = x.astype(BF16)
    r = x - hi.astype(F32)
    mid = r.astype(BF16)
    lo = (r - mid.astype(F32)).astype(BF16)
    return hi, mid, lo


def _hgrn_kernel(q_ref, f_ref, i_ref, og_ref, lb_ref, gain_ref, o_ref, state_ref, *, rows, eps):
    @pl.when(pl.program_id(2) == 0)
    def _():
        state_ref[...] = jnp.zeros_like(state_ref)

    dk = q_ref.shape[-1]
    dv = i_ref.shape[-1]
    lb = lb_ref[...]
    log_lb = jnp.log(lb)
    log_1m_lb = jnp.log1p(-lb)
    one_m_lb = 1.0 - lb
    gain = gain_ref[...]
    r64 = lax.broadcasted_iota(jnp.int32, (HG_CHUNK, HG_CHUNK), 0)
    c64 = lax.broadcasted_iota(jnp.int32, (HG_CHUNK, HG_CHUNK), 1)
    tril = jnp.where((c64 <= r64) & (r64 // SUB == c64 // SUB), 1.0, 0.0).astype(BF16)
    ones_k = jnp.ones((dk, LANES), BF16)
    ones_v = jnp.full((dv, LANES), 1.0 / dv, F32).astype(BF16)
    row_id = lax.broadcasted_iota(jnp.int32, (SUB, dk), 0)

    def chunk_body(ci, st):
        r0 = pl.multiple_of(ci * HG_CHUNK, HG_CHUNK)
        z = f_ref[pl.ds(r0, HG_CHUNK), :]
        q_raw = q_ref[pl.ds(r0, HG_CHUNK), :]
        v = i_ref[pl.ds(r0, HG_CHUNK), :]
        og = og_ref[pl.ds(r0, HG_CHUNK), :]
        y = log_1m_lb + (jnp.minimum(z, 0.0) - jnp.log1p(jnp.exp(-jnp.abs(z))))
        g = jnp.maximum(log_lb, y) + jnp.log1p(jnp.exp(-jnp.abs(log_lb - y)))
        kk = one_m_lb / (1.0 + jnp.exp(z))
        q = _silu(q_raw)
        g1, g2, g3 = _split3(g)
        b = (jnp.dot(tril, g1, preferred_element_type=F32) + jnp.dot(tril, g2, preferred_element_type=F32)
             + jnp.dot(tril, g3, preferred_element_type=F32))
        outs = []
        for j in range(HG_CHUNK // SUB):
            sl = slice(SUB * j, SUB * (j + 1))
            bj, qj, kj, vj = b[sl], q[sl], kk[sl], v[sl]
            b_last = bj[SUB - 1:SUB, :]
            o_inter = lax.dot_general((qj * jnp.exp(bj)).astype(BF16), st.astype(BF16),
                                      (((1,), (1,)), ((), ())), preferred_element_type=F32)
            terms = []
            for s in range(SUB):
                d = jnp.minimum(bj - bj[s:s + 1, :], 0.0)
                w = jnp.exp(d) * (qj * kj[s:s + 1, :])
                terms.append(jnp.where(row_id >= s, w, 0.0).astype(BF16))
            scores = jnp.dot(jnp.concatenate(terms, axis=0), ones_k, preferred_element_type=F32)
            o = o_inter
            for s in range(SUB):
                o = o + scores[SUB * s:SUB * (s + 1), :dv] * vj[s:s + 1, :]
            k_dec = kj * jnp.exp(b_last - bj)
            st = st * jnp.exp(b_last) + lax.dot_general(vj.astype(BF16), k_dec.astype(BF16),
                                                        (((0,), (0,)), ((), ())), preferred_element_type=F32)
            outs.append(o)
        o = jnp.concatenate(outs, axis=0)
        ms = jnp.dot((o * o).astype(BF16), ones_v, preferred_element_type=F32)[:, :dv]
        o = o * lax.rsqrt(ms + eps) * gain * _silu(og)
        o_ref[pl.ds(r0, HG_CHUNK), :] = o.astype(o_ref.dtype)
        return st

    state_ref[...] = lax.fori_loop(0, rows // HG_CHUNK, chunk_body, state_ref[...])


def hgrn2(hg, lb, gain, *, heads, dk, dv, rows, eps):
    b, s, _ = hg.shape
    assert dk == LANES and dv == LANES and s % rows == 0 and rows % HG_CHUNK == 0

    def col(group):
        return pl.BlockSpec((None, rows, dk), lambda bi, hi, si: (bi, si, group * heads + hi))

    vec = pl.BlockSpec((1, dk), lambda bi, hi, si: (0, hi))
    return pl.pallas_call(
        functools.partial(_hgrn_kernel, rows=rows, eps=eps),
        grid=(b, heads, s // rows),
        in_specs=[col(0), col(1), col(2), col(3), vec, vec],
        out_specs=pl.BlockSpec((None, rows, dv), lambda bi, hi, si: (bi, si, hi)),
        out_shape=jax.ShapeDtypeStruct((b, s, heads * dv), BF16),
        scratch_shapes=[pltpu.VMEM((dv, dk), F32)],
        compiler_params=_cparams(("parallel", "parallel", "arbitrary")),
        name="hgrn2",
    )(hg, hg, hg, hg, lb.reshape(1, -1), gain.reshape(1, -1))


def _router_kernel(x_ref, g_ref, wr_ref, h_ref, comb_ref, *, n_experts, eps):
    x = x_ref[...]
    h = x * lax.rsqrt(jnp.mean(x * x, axis=-1, keepdims=True) + eps) * g_ref[...]
    h_ref[...] = h.astype(h_ref.dtype)
    logits = jnp.dot(h, wr_ref[...], preferred_element_type=F32, precision=lax.Precision.HIGHEST)
    lane = lax.broadcasted_iota(jnp.int32, logits.shape, 1)
    logits = jnp.where(lane < n_experts, logits, -jnp.inf)
    m1 = jnp.max(logits, axis=1, keepdims=True)
    i1 = jnp.min(jnp.where(logits == m1, lane, LANES), axis=1, keepdims=True)
    rest = jnp.where(lane == i1, -jnp.inf, logits)
    m2 = jnp.max(rest, axis=1, keepdims=True)
    i2 = jnp.min(jnp.where(rest == m2, lane, LANES), axis=1, keepdims=True)
    e2 = jnp.exp(m2 - m1)
    w1 = 1.0 / (1.0 + e2)
    w2 = e2 / (1.0 + e2)
    comb_ref[...] = jnp.where(lane == i1, w1, 0.0) + jnp.where(lane == i2, w2, 0.0)


def router(x, g, w_router, *, tm, eps):
    n, d = x.shape
    n_experts = w_router.shape[1]
    wr = jnp.zeros((d, LANES), F32).at[:, :n_experts].set(w_router)
    return pl.pallas_call(
        functools.partial(_router_kernel, n_experts=n_experts, eps=eps),
        grid=(n // tm,),
        in_specs=[pl.BlockSpec((tm, d), lambda i: (i, 0)), pl.BlockSpec((1, d), lambda i: (0, 0)),
                  pl.BlockSpec((d, LANES), lambda i: (0, 0))],
        out_specs=[pl.BlockSpec((tm, d), lambda i: (i, 0)), pl.BlockSpec((tm, LANES), lambda i: (i, 0))],
        out_shape=[jax.ShapeDtypeStruct((n, d), BF16), jax.ShapeDtypeStruct((n, LANES), F32)],
        compiler_params=_cparams(("parallel",)),
        name="ffn_norm_router",
    )(x, g.reshape(1, d), wr)


def _rope_tables(positions, dm):
    half = dm.mla_rope // 2
    inv_freq = dm.rope_theta ** (-jnp.arange(half, dtype=F32) / half)
    ang = positions.astype(F32).reshape(-1, 1) * inv_freq
    cos, sin = jnp.cos(ang), jnp.sin(ang)
    zero = jnp.zeros_like(cos)
    c = jnp.concatenate([cos, cos, zero, zero], axis=-1)
    s1 = jnp.concatenate([zero, sin, zero, zero], axis=-1)
    s2 = jnp.concatenate([-sin, zero, zero, zero], axis=-1)
    return c, s1, s2


def _mixer(x, lp, tables, dm):
    n = dm.batch * dm.seq
    d = dm.d_model
    hg_cols = 4 * dm.hg_heads * dm.hg_dk
    c, s1, s2 = tables
    h = rmsnorm(x, lp["norm_mix"], BF16, tm=dm.tm_norm, eps=dm.eps)

    (hg,) = matmul([h], [lp["w_hg"]], [0], [], [], [jax.ShapeDtypeStruct((n, hg_cols), F32)],
                   [_tile_spec(dm.tm, dm.tn)], _ep_store, tm=dm.tm, tn=dm.tn, name="proj_hgrn")
    (gates,) = matmul([h], [lp["w_gates"]], [0], [], [], [jax.ShapeDtypeStruct((n, 2 * d), BF16)],
                      [_tile_spec(dm.tm, dm.tn)], _ep_store, tm=dm.tm, tn=dm.tn, name="proj_gates")

    mla_cols = lp["w_mla"].shape[1]
    tms = dm.tm_small
    cq, ckv, kpe = matmul(
        [h], [lp["w_mla"]], [0],
        [lp["q_gain"], lp["kv_gain"], c, s1, s2],
        [pl.BlockSpec((1, dm.mla_q_rank), lambda i, j, k: (0, 0)),
         pl.BlockSpec((1, dm.mla_kv_rank), lambda i, j, k: (0, 0)),
         _row_spec(tms, LANES), _row_spec(tms, LANES), _row_spec(tms, LANES)],
        [jax.ShapeDtypeStruct((n, dm.mla_q_rank), BF16), jax.ShapeDtypeStruct((n, dm.mla_kv_rank), BF16),
         jax.ShapeDtypeStruct((n, LANES), BF16)],
        [_row_spec(tms, dm.mla_q_rank), _row_spec(tms, dm.mla_kv_rank), _row_spec(tms, LANES)],
        functools.partial(_ep_mla_pre, q_rank=dm.mla_q_rank, kv_rank=dm.mla_kv_rank, eps=dm.eps),
        tm=tms, tn=mla_cols, name="proj_mla_latents")

    heads = dm.mla_heads
    hb = min(4, heads)
    s_tiles = dm.seq // tms

    def head_spec(width):
        return pl.BlockSpec((None, hb, tms, width), lambda i, j, k: (i // s_tiles, j, i % s_tiles, 0))

    scale = (dm.mla_nope + dm.mla_rope) ** -0.5
    (q_full,) = matmul(
        [cq], [lp["w_uq"]], [0], [c, s1, s2],
        [_row_spec(tms, LANES), _row_spec(tms, LANES), _row_spec(tms, LANES)],
        [jax.ShapeDtypeStruct((dm.batch, heads, dm.seq, 2 * LANES), BF16)], [head_spec(2 * LANES)],
        functools.partial(_ep_mla_q, heads=hb, scale=scale), tm=tms, tn=hb * 2 * LANES, name="mla_q_up")
    k_full, v = matmul(
        [ckv], [lp["w_ukv"]], [0], [kpe], [_row_spec(tms, LANES)],
        [jax.ShapeDtypeStruct((dm.batch, heads, dm.seq, 2 * LANES), BF16),
         jax.ShapeDtypeStruct((dm.batch, heads, dm.seq, dm.mla_dv), BF16)],
        [head_spec(2 * LANES), head_spec(dm.mla_dv)],
        functools.partial(_ep_mla_kv, heads=hb), tm=tms, tn=hb * 2 * LANES, name="mla_kv_up")
    o_b = attention(q_full, k_full, v, tq=dm.tq, chunk=dm.chunk).reshape(n, heads * dm.mla_dv)

    o_a = hgrn2(hg.reshape(dm.batch, dm.seq, hg_cols), lp["lb"], lp["hg_gain"], heads=dm.hg_heads, dk=dm.hg_dk,
                dv=dm.hg_dv, rows=dm.hgrn_rows, eps=dm.eps).reshape(n, dm.hg_heads * dm.hg_dv)

    (y,) = matmul([o_a, o_b], [lp["w_branch_a"], lp["w_branch_b"]], [0, 1], [gates, gates],
                  [_tile_spec(dm.tm, dm.tn2), _tile_spec(dm.tm, dm.tn2, d // dm.tn2)],
                  [jax.ShapeDtypeStruct((n, d), BF16)], [_tile_spec(dm.tm, dm.tn2)], _ep_gated_sum,
                  tm=dm.tm, tn=dm.tn2, name="branch_merge")
    (x,) = matmul([y], [lp["w_out"]], [0], [x], [_tile_spec(dm.tm, dm.tn2)],
                  [jax.ShapeDtypeStruct((n, d), F32)], [_tile_spec(dm.tm, dm.tn2)], _ep_residual,
                  tm=dm.tm, tn=dm.tn2, name="mixer_out")
    return x


def _dense_ffn(x, lp, dm):
    n, d = x.shape
    h = rmsnorm(x, lp["norm_ffn"], BF16, tm=dm.tm_norm, eps=dm.eps)
    (u,) = matmul([h], [lp["w1"], lp["w3"]], [0, 0], [], [], [jax.ShapeDtypeStruct((n, dm.d_ff_pad), BF16)],
                  [_tile_spec(dm.tm, dm.tn2)], _ep_swiglu, tm=dm.tm, tn=dm.tn2, name="ffn_up")
    (x,) = matmul([u], [lp["w2"]], [0], [x], [_tile_spec(dm.tm, dm.tn)], [jax.ShapeDtypeStruct((n, d), F32)],
                  [_tile_spec(dm.tm, dm.tn)], _ep_residual, tm=dm.tm, tn=dm.tn, tk=dm.tk_ffn, name="ffn_down")
    return x


def _moe_ffn(x, lp, dm):
    n, d = x.shape
    h, comb = router(x, lp["norm_ffn"], lp["w_router"], tm=dm.tm_norm, eps=dm.eps)
    for e in range(dm.n_experts):
        (u,) = matmul([h], [lp["w1"][e], lp["w3"][e]], [0, 0], [], [],
                      [jax.ShapeDtypeStruct((n, dm.d_ff_expert), BF16)], [_tile_spec(dm.tm, dm.tn2)], _ep_swiglu,
                      tm=dm.tm, tn=dm.tn2, name="moe_up")
        (x,) = matmul([u], [lp["w2"][e]], [0], [x, comb], [_tile_spec(dm.tm, dm.tn2), _row_spec(dm.tm, LANES)],
                      [jax.ShapeDtypeStruct((n, d), F32)], [_tile_spec(dm.tm, dm.tn2)],
                      functools.partial(_ep_expert_accumulate, expert=e), tm=dm.tm, tn=dm.tn2, name="moe_down")
    return x


def _prepare_layer(l, p, lbs, dm):
    d = dm.d_model
    hg_cols = 4 * dm.hg_heads * dm.hg_dk
    mla_in = dm.mla_q_rank + dm.mla_kv_rank + dm.mla_rope
    w_in = p["w_in"][l]
    w_mla = jnp.pad(w_in[:, hg_cols:hg_cols + mla_in], ((0, 0), (0, LANES - dm.mla_rope)))
    heads = dm.mla_heads
    w_uq = p["w_uq"][l].reshape(dm.mla_q_rank, heads, dm.mla_nope + dm.mla_rope)
    w_uq = jnp.pad(w_uq, ((0, 0), (0, 0), (0, 2 * LANES - dm.mla_nope - dm.mla_rope)))
    hg_width = dm.hg_heads * dm.hg_dv
    lp = {
        "norm_mix": p["norm_mix"][l],
        "w_hg": w_in[:, :hg_cols].astype(BF16),
        "w_mla": w_mla.astype(BF16),
        "w_gates": w_in[:, hg_cols + mla_in:].astype(BF16),
        "lb": lbs[l],
        "hg_gain": p["hg_norm"][l],
        "q_gain": p["mla_q_norm"][l].reshape(1, -1),
        "kv_gain": p["mla_kv_norm"][l].reshape(1, -1),
        "w_uq": w_uq.reshape(dm.mla_q_rank, heads * 2 * LANES).astype(BF16),
        "w_ukv": p["w_ukv"][l].astype(BF16),
        "w_branch_a": p["w_branch"][l, :hg_width].astype(BF16),
        "w_branch_b": p["w_branch"][l, hg_width:].astype(BF16),
        "w_out": p["w_out"][l].astype(BF16),
        "norm_ffn": p["norm_ffn"][l],
    }
    if l % 2 == 0:
        pad = dm.d_ff_pad - dm.d_ff
        lp["w1"] = jnp.pad(p["ffn_w1"][l // 2], ((0, 0), (0, pad))).astype(BF16)
        lp["w3"] = jnp.pad(p["ffn_w3"][l // 2], ((0, 0), (0, pad))).astype(BF16)
        lp["w2"] = jnp.pad(p["ffn_w2"][l // 2], ((0, pad), (0, 0))).astype(BF16)
    else:
        lp["w_router"] = p["w_router"][l // 2]
        lp["w1"] = p["moe_w1"][l // 2].astype(BF16)
        lp["w3"] = p["moe_w3"][l // 2].astype(BF16)
        lp["w2"] = p["moe_w2"][l // 2].astype(BF16)
    return lp


def forward(p, dm):
    n = dm.batch * dm.seq
    x = p["x"].reshape(n, dm.d_model)
    tables = _rope_tables(p["positions"], dm)
    lbs = jnp.cumsum(jax.nn.softmax(p["hg_lb_logits"].astype(F32), axis=0), axis=0)
    lbs = lbs - lbs[0:1]
    for l in range(dm.depth):
        lp = _prepare_layer(l, p, lbs, dm)
        x = _mixer(x, lp, tables, dm)
        x = _dense_ffn(x, lp, dm) if l % 2 == 0 else _moe_ffn(x, lp, dm)
    out = rmsnorm(x, p["norm_final"], F32, tm=dm.tm_norm, eps=dm.eps)
    return out.reshape(dm.batch, dm.seq, dm.d_model)


def kernel(x, positions, norm_mix, w_in, hg_lb_logits, hg_norm, mla_q_norm, w_uq, mla_kv_norm, w_ukv, w_branch, w_out, norm_ffn, ffn_w1, ffn_w3, ffn_w2, w_router, moe_w1, moe_w3, moe_w2, norm_final):
    p = dict(x=x, positions=positions, norm_mix=norm_mix, w_in=w_in, hg_lb_logits=hg_lb_logits, hg_norm=hg_norm,
             mla_q_norm=mla_q_norm, w_uq=w_uq, mla_kv_norm=mla_kv_norm, w_ukv=w_ukv, w_branch=w_branch, w_out=w_out,
             norm_ffn=norm_ffn, ffn_w1=ffn_w1, ffn_w3=ffn_w3, ffn_w2=ffn_w2, w_router=w_router, moe_w1=moe_w1,
             moe_w3=moe_w3, moe_w2=moe_w2, norm_final=norm_final)
    return forward(p, Dims())
```

```python
import functools
from typing import NamedTuple

import jax
import jax.numpy as jnp
from jax import lax
from jax.experimental import pallas as pl
from jax.experimental.pallas import tpu as pltpu

F32 = jnp.float32
BF16 = jnp.bfloat16

LANES = 128
V7X_VMEM_BYTES = 64 * 1024 * 1024
VMEM_LIMIT_BYTES = V7X_VMEM_BYTES - 8 * 1024 * 1024


class Dims(NamedTuple):
    d_model: int = 4096
    batch: int = 2
    seq: int = 8192
    depth: int = 2
    chunk: int = 64
    eps: float = 1e-6
    hg_dk: int = 128
    hg_heads: int = 16
    hg_dv: int = 128
    mla_dv: int = 128
    mla_heads: int = 16
    mla_nope: int = 128
    mla_rope: int = 64
    mla_q_rank: int = 768
    mla_kv_rank: int = 512
    rope_theta: float = 10000.0
    d_ff: int = 11008
    n_experts: int = 8
    d_ff_expert: int = 4096
    tm: int = 1024
    tn: int = 1024
    tn2: int = 512
    tk_ffn: int = 2816
    d_ff_pad: int = 11264
    tm_small: int = 512
    tm_norm: int = 256
    tq: int = 1024
    hgrn_rows: int = 512


LOG2_E = 1.4426950408889634
SUB = 16
HG_CHUNK = 64


def _cparams(semantics):
    return pltpu.CompilerParams(dimension_semantics=semantics, vmem_limit_bytes=VMEM_LIMIT_BYTES)


def _sigmoid(x):
    return 1.0 / (1.0 + jnp.exp(-x))


def _silu(x):
    return x * _sigmoid(x)


def _rmsnorm_kernel(x_ref, g_ref, o_ref, *, eps):
    x = x_ref[...]
    ms = jnp.mean(x * x, axis=-1, keepdims=True)
    o_ref[...] = (x * lax.rsqrt(ms + eps) * g_ref[...]).astype(o_ref.dtype)


def rmsnorm(x, g, out_dtype, *, tm, eps):
    n, d = x.shape
    return pl.pallas_call(
        functools.partial(_rmsnorm_kernel, eps=eps),
        grid=(n // tm,),
        in_specs=[pl.BlockSpec((tm, d), lambda i: (i, 0)), pl.BlockSpec((1, d), lambda i: (0, 0))],
        out_specs=pl.BlockSpec((tm, d), lambda i: (i, 0)),
        out_shape=jax.ShapeDtypeStruct((n, d), out_dtype),
        compiler_params=_cparams(("parallel",)),
        name="rmsnorm",
    )(x, g.reshape(1, d))


def _mm_kernel(*refs, n_a, w_src, n_extra, n_out, nk, epilogue):
    n_w = len(w_src)
    a_refs = refs[:n_a]
    w_refs = refs[n_a:n_a + n_w]
    e_refs = refs[n_a + n_w:n_a + n_w + n_extra]
    o_refs = refs[n_a + n_w + n_extra:n_a + n_w + n_extra + n_out]
    acc_refs = refs[n_a + n_w + n_extra + n_out:]
    parts = [jnp.dot(a_refs[s][...], w[...], preferred_element_type=F32) for w, s in zip(w_refs, w_src)]
    if nk == 1:
        epilogue(parts, e_refs, o_refs)
        return
    k = pl.program_id(2)

    @pl.when(k == 0)
    def _():
        for acc, p in zip(acc_refs, parts):
            acc[...] = p

    @pl.when(k > 0)
    def _():
        for acc, p in zip(acc_refs, parts):
            acc[...] += p

    @pl.when(k == nk - 1)
    def _():
        epilogue([acc[...] for acc in acc_refs], e_refs, o_refs)


def matmul(a_list, w_list, w_src, extras, extra_specs, out_shapes, out_specs, epilogue, *, tm, tn, tk=None, name):
    m, kdim = a_list[0].shape
    n = w_list[0].shape[1]
    tk = kdim if tk is None else tk
    nk = kdim // tk
    assert m % tm == 0 and n % tn == 0 and kdim % tk == 0
    in_specs = [pl.BlockSpec((tm, tk), lambda i, j, k: (i, k)) for _ in a_list]
    in_specs += [pl.BlockSpec((tk, tn), lambda i, j, k: (k, j)) for _ in w_list]
    in_specs += list(extra_specs)
    scratch = [pltpu.VMEM((tm, tn), F32) for _ in w_list] if nk > 1 else []
    return pl.pallas_call(
        functools.partial(_mm_kernel, n_a=len(a_list), w_src=tuple(w_src), n_extra=len(extras),
                          n_out=len(out_shapes), nk=nk, epilogue=epilogue),
        grid=(m // tm, n // tn, nk),
        in_specs=in_specs,
        out_specs=list(out_specs),
        out_shape=list(out_shapes),
        scratch_shapes=scratch,
        compiler_params=_cparams(("parallel", "parallel", "arbitrary")),
        name=name,
    )(*a_list, *w_list, *extras)


def _tile_spec(tm, tn, col_block_offset=0):
    return pl.BlockSpec((tm, tn), lambda i, j, k: (i, j + col_block_offset))


def _row_spec(tm, width):
    return pl.BlockSpec((tm, width), lambda i, j, k: (i, 0))


def _ep_store(parts, e_refs, o_refs):
    o_refs[0][...] = parts[0].astype(o_refs[0].dtype)


def _ep_residual(parts, e_refs, o_refs):
    o_refs[0][...] = e_refs[0][...] + parts[0]


def _ep_swiglu(parts, e_refs, o_refs):
    o_refs[0][...] = (_silu(parts[0]) * parts[1]).astype(o_refs[0].dtype)


def _ep_gated_sum(parts, e_refs, o_refs):
    ga = e_refs[0][...].astype(F32)
    gb = e_refs[1][...].astype(F32)
    o_refs[0][...] = (_sigmoid(ga) * parts[0] + _sigmoid(gb) * parts[1]).astype(o_refs[0].dtype)


def _ep_expert_accumulate(parts, e_refs, o_refs, *, expert):
    weight = e_refs[1][...][:, expert:expert + 1]
    o_refs[0][...] = e_refs[0][...] + weight * parts[0]


def _rope128(g, c, s1, s2):
    return g * c + pltpu.roll(g, 32, 1) * s1 + pltpu.roll(g, LANES - 32, 1) * s2


def _ep_mla_pre(parts, e_refs, o_refs, *, q_rank, kv_rank, eps):
    acc = parts[0]
    qg, kvg, c, s1, s2 = (r[...] for r in e_refs)
    cq = acc[:, :q_rank]
    ckv = acc[:, q_rank:q_rank + kv_rank]
    kp = acc[:, q_rank + kv_rank:q_rank + kv_rank + LANES]
    cq = cq * lax.rsqrt(jnp.mean(cq * cq, axis=-1, keepdims=True) + eps) * qg
    ckv = ckv * lax.rsqrt(jnp.mean(ckv * ckv, axis=-1, keepdims=True) + eps) * kvg
    o_refs[0][...] = cq.astype(o_refs[0].dtype)
    o_refs[1][...] = ckv.astype(o_refs[1].dtype)
    o_refs[2][...] = _rope128(kp, c, s1, s2).astype(o_refs[2].dtype)


def _ep_mla_q(parts, e_refs, o_refs, *, heads, scale):
    acc = parts[0]
    c, s1, s2 = (r[...] for r in e_refs)
    for h in range(heads):
        base = 2 * LANES * h
        o_refs[0][h, :, :LANES] = (acc[:, base:base + LANES] * scale).astype(o_refs[0].dtype)
        pe = _rope128(acc[:, base + LANES:base + 2 * LANES], c, s1, s2)
        o_refs[0][h, :, LANES:] = (pe * scale).astype(o_refs[0].dtype)


def _ep_mla_kv(parts, e_refs, o_refs, *, heads):
    acc = parts[0]
    kpe = e_refs[0][...]
    for h in range(heads):
        base = 2 * LANES * h
        o_refs[0][h, :, :LANES] = acc[:, base:base + LANES].astype(o_refs[0].dtype)
        o_refs[0][h, :, LANES:] = kpe
        o_refs[1][h] = acc[:, base + LANES:base + 2 * LANES].astype(o_refs[1].dtype)


ATTN_SPLIT = 2


def _attn_kernel(q_ref, k_ref, v_ref, o_ref, *, tq, chunk):
    qi = pl.program_id(2)
    dv = v_ref.shape[-1]
    rows = tq // ATTN_SPLIT
    qs = [q_ref[pl.ds(i * rows, rows), :] for i in range(ATTN_SPLIT)]
    r_chunk = lax.broadcasted_iota(jnp.int32, (rows, rows), 0) // chunk
    c_chunk = lax.broadcasted_iota(jnp.int32, (rows, rows), 1) // chunk
    diag_mask = c_chunk <= r_chunk

    def update(q, k, v, carry, mask):
        m, l, acc = carry
        s = lax.dot_general(q, k, (((1,), (1,)), ((), ())), preferred_element_type=F32)
        if mask is not None:
            s = jnp.where(mask, s, -jnp.inf)
        m_new = jnp.maximum(m, jnp.max(s, axis=1, keepdims=True))
        p = jnp.exp2(s - m_new[:, :1])
        alpha = jnp.exp2(m - m_new)
        l = alpha * l + jnp.sum(p, axis=1, keepdims=True)
        acc = alpha[:, :dv] * acc + jnp.dot(p.astype(BF16), v, preferred_element_type=F32)
        return m_new, l, acc

    def full_block(kb, carry):
        ks = pl.multiple_of(kb * tq, tq)
        k = k_ref[pl.ds(ks, tq), :]
        v = v_ref[pl.ds(ks, tq), :]
        return tuple(update(qs[i], k, v, carry[i], None) for i in range(ATTN_SPLIT))

    init = tuple((jnp.full((rows, LANES), -jnp.inf, F32), jnp.zeros((rows, LANES), F32), jnp.zeros((rows, dv), F32))
                 for _ in range(ATTN_SPLIT))
    carry = list(lax.fori_loop(0, qi, full_block, init))
    for i in range(ATTN_SPLIT):
        for d in range(i + 1):
            ks = pl.multiple_of(qi * tq + d * rows, rows)
            k = k_ref[pl.ds(ks, rows), :]
            v = v_ref[pl.ds(ks, rows), :]
            carry[i] = update(qs[i], k, v, carry[i], diag_mask if d == i else None)
    for i in range(ATTN_SPLIT):
        _, l, acc = carry[i]
        o_ref[pl.ds(i * rows, rows), :] = (acc / l[:, :dv]).astype(o_ref.dtype)


def attention(q_full, k_full, v, *, tq, chunk):
    b, h, s, dq = q_full.shape
    dv = v.shape[-1]
    assert s % tq == 0 and (tq // ATTN_SPLIT) % chunk == 0 and dv == LANES
    return pl.pallas_call(
        functools.partial(_attn_kernel, tq=tq, chunk=chunk),
        grid=(b, h, s // tq),
        in_specs=[
            pl.BlockSpec((None, None, tq, dq), lambda bi, hi, qi: (bi, hi, qi, 0)),
            pl.BlockSpec((None, None, s, dq), lambda bi, hi, qi: (bi, hi, 0, 0)),
            pl.BlockSpec((None, None, s, dv), lambda bi, hi, qi: (bi, hi, 0, 0)),
        ],
        out_specs=pl.BlockSpec((None, tq, dv), lambda bi, hi, qi: (bi, qi, hi)),
        out_shape=jax.ShapeDtypeStruct((b, s, h * dv), BF16),
        compiler_params=_cparams(("parallel", "parallel", "arbitrary")),
        name="mla_attention",
    )(q_full, k_full, v)


def _split3(x):
    hi = x.astype(BF16)
    r = x - hi.astype(F32)
    mid = r.astype(BF16)
    lo = (r - mid.astype(F32)).astype(BF16)
    return hi, mid, lo


def _hgrn_kernel(q_ref, f_ref, i_ref, og_ref, lb_ref, gain_ref, o_ref, state_ref, *, rows, eps):
    @pl.when(pl.program_id(2) == 0)
    def _():
        state_ref[...] = jnp.zeros_like(state_ref)

    dk = q_ref.shape[-1]
    dv = i_ref.shape[-1]
    lb = lb_ref[...]
    log_lb = jnp.log(lb)
    log_1m_lb = jnp.log1p(-lb)
    one_m_lb = 1.0 - lb
    gain = gain_ref[...]
    r64 = lax.broadcasted_iota(jnp.int32, (HG_CHUNK, HG_CHUNK), 0)
    c64 = lax.broadcasted_iota(jnp.int32, (HG_CHUNK, HG_CHUNK), 1)
    tril = jnp.where((c64 <= r64) & (r64 // SUB == c64 // SUB), 1.0, 0.0).astype(BF16)
    ones_k = jnp.ones((dk, LANES), BF16)
    ones_v = jnp.full((dv, LANES), 1.0 / dv, F32).astype(BF16)
    row_id = lax.broadcasted_iota(jnp.int32, (SUB, dk), 0)
    groups = [slice(SUB * j, SUB * (j + 1)) for j in range(rows // SUB)]

    z = f_ref[...]
    y = log_1m_lb + (jnp.minimum(z, 0.0) - jnp.log1p(jnp.exp(-jnp.abs(z))))
    g = jnp.maximum(log_lb, y) + jnp.log1p(jnp.exp(-jnp.abs(log_lb - y)))
    kk = one_m_lb / (1.0 + jnp.exp(z))
    q = _silu(q_ref[...])
    v = i_ref[...]
    g1, g2, g3 = _split3(g)
    b = jnp.concatenate([
        jnp.dot(tril, g1[c:c + HG_CHUNK], preferred_element_type=F32)
        + jnp.dot(tril, g2[c:c + HG_CHUNK], preferred_element_type=F32)
        + jnp.dot(tril, g3[c:c + HG_CHUNK], preferred_element_type=F32) for c in range(0, rows, HG_CHUNK)], axis=0)

    scores = []
    for sl in groups:
        bj, qj, kj = b[sl], q[sl], kk[sl]
        terms = []
        for s in range(SUB):
            d = jnp.minimum(bj - bj[s:s + 1, :], 0.0)
            w = jnp.exp(d) * (qj * kj[s:s + 1, :])
            terms.append(jnp.where(row_id >= s, w, 0.0).astype(BF16))
        scores.append(jnp.dot(jnp.concatenate(terms, axis=0), ones_k, preferred_element_type=F32))

    incs, decs = [], []
    for sl in groups:
        bj = b[sl]
        b_last = bj[SUB - 1:SUB, :]
        k_dec = kk[sl] * jnp.exp(b_last - bj)
        incs.append(lax.dot_general(v[sl].astype(BF16), k_dec.astype(BF16), (((0,), (0,)), ((), ())),
                                    preferred_element_type=F32))
        decs.append(jnp.exp(b_last))

    st = state_ref[...]
    q_dec = (q * jnp.exp(b)).astype(BF16)
    o_inter = []
    for j, sl in enumerate(groups):
        o_inter.append(lax.dot_general(q_dec[sl], st.astype(BF16), (((1,), (1,)), ((), ())),
                                       preferred_element_type=F32))
        st = st * decs[j] + incs[j]
    state_ref[...] = st

    outs = []
    for j, sl in enumerate(groups):
        vj = v[sl]
        o = o_inter[j]
        for s in range(SUB):
            o = o + scores[j][SUB * s:SUB * (s + 1), :dv] * vj[s:s + 1, :]
        outs.append(o)
    o = jnp.concatenate(outs, axis=0)
    ms = jnp.dot((o * o).astype(BF16), ones_v, preferred_element_type=F32)[:, :dv]
    o = o * lax.rsqrt(ms + eps) * gain * _silu(og_ref[...])
    o_ref[...] = o.astype(o_ref.dtype)


def hgrn2(hg, lb, gain, *, heads, dk, dv, rows, eps):
    b, s, _ = hg.shape
    assert dk == LANES and dv == LANES and s % rows == 0 and rows % HG_CHUNK == 0

    def col(group):
        return pl.BlockSpec((None, rows, dk), lambda bi, hi, si: (bi, si, group * heads + hi))

    vec = pl.BlockSpec((1, dk), lambda bi, hi, si: (0, hi))
    return pl.pallas_call(
        functools.partial(_hgrn_kernel, rows=rows, eps=eps),
        grid=(b, heads, s // rows),
        in_specs=[col(0), col(1), col(2), col(3), vec, vec],
        out_specs=pl.BlockSpec((None, rows, dv), lambda bi, hi, si: (bi, si, hi)),
        out_shape=jax.ShapeDtypeStruct((b, s, heads * dv), BF16),
        scratch_shapes=[pltpu.VMEM((dv, dk), F32)],
        compiler_params=_cparams(("parallel", "parallel", "arbitrary")),
        name="hgrn2",
    )(hg, hg, hg, hg, lb.reshape(1, -1), gain.reshape(1, -1))


def _router_kernel(x_ref, g_ref, wr_ref, h_ref, comb_ref, *, n_experts, eps):
    x = x_ref[...]
    h = x * lax.rsqrt(jnp.mean(x * x, axis=-1, keepdims=True) + eps) * g_ref[...]
    h_ref[...] = h.astype(h_ref.dtype)
    logits = jnp.dot(h, wr_ref[...], preferred_element_type=F32, precision=lax.Precision.HIGHEST)
    lane = lax.broadcasted_iota(jnp.int32, logits.shape, 1)
    logits = jnp.where(lane < n_experts, logits, -jnp.inf)
    m1 = jnp.max(logits, axis=1, keepdims=True)
    i1 = jnp.min(jnp.where(logits == m1, lane, LANES), axis=1, keepdims=True)
    rest = jnp.where(lane == i1, -jnp.inf, logits)
    m2 = jnp.max(rest, axis=1, keepdims=True)
    i2 = jnp.min(jnp.where(rest == m2, lane, LANES), axis=1, keepdims=True)
    e2 = jnp.exp(m2 - m1)
    w1 = 1.0 / (1.0 + e2)
    w2 = e2 / (1.0 + e2)
    comb_ref[...] = jnp.where(lane == i1, w1, 0.0) + jnp.where(lane == i2, w2, 0.0)


def router(x, g, w_router, *, tm, eps):
    n, d = x.shape
    n_experts = w_router.shape[1]
    wr = jnp.zeros((d, LANES), F32).at[:, :n_experts].set(w_router)
    return pl.pallas_call(
        functools.partial(_router_kernel, n_experts=n_experts, eps=eps),
        grid=(n // tm,),
        in_specs=[pl.BlockSpec((tm, d), lambda i: (i, 0)), pl.BlockSpec((1, d), lambda i: (0, 0)),
                  pl.BlockSpec((d, LANES), lambda i: (0, 0))],
        out_specs=[pl.BlockSpec((tm, d), lambda i: (i, 0)), pl.BlockSpec((tm, LANES), lambda i: (i, 0))],
        out_shape=[jax.ShapeDtypeStruct((n, d), BF16), jax.ShapeDtypeStruct((n, LANES), F32)],
        compiler_params=_cparams(("parallel",)),
        name="ffn_norm_router",
    )(x, g.reshape(1, d), wr)


def _rope_tables(positions, dm):
    half = dm.mla_rope // 2
    inv_freq = dm.rope_theta ** (-jnp.arange(half, dtype=F32) / half)
    ang = positions.astype(F32).reshape(-1, 1) * inv_freq
    cos, sin = jnp.cos(ang), jnp.sin(ang)
    zero = jnp.zeros_like(cos)
    c = jnp.concatenate([cos, cos, zero, zero], axis=-1)
    s1 = jnp.concatenate([zero, sin, zero, zero], axis=-1)
    s2 = jnp.concatenate([-sin, zero, zero, zero], axis=-1)
    return c, s1, s2


def _mixer(x, lp, tables, dm):
    n = dm.batch * dm.seq
    d = dm.d_model
    hg_cols = 4 * dm.hg_heads * dm.hg_dk
    c, s1, s2 = tables
    h = rmsnorm(x, lp["norm_mix"], BF16, tm=dm.tm_norm, eps=dm.eps)

    (hg,) = matmul([h], [lp["w_hg"]], [0], [], [], [jax.ShapeDtypeStruct((n, hg_cols), F32)],
                   [_tile_spec(dm.tm, dm.tn)], _ep_store, tm=dm.tm, tn=dm.tn, name="proj_hgrn")
    (gates,) = matmul([h], [lp["w_gates"]], [0], [], [], [jax.ShapeDtypeStruct((n, 2 * d), BF16)],
                      [_tile_spec(dm.tm, dm.tn)], _ep_store, tm=dm.tm, tn=dm.tn, name="proj_gates")

    mla_cols = lp["w_mla"].shape[1]
    tms = dm.tm_small
    cq, ckv, kpe = matmul(
        [h], [lp["w_mla"]], [0],
        [lp["q_gain"], lp["kv_gain"], c, s1, s2],
        [pl.BlockSpec((1, dm.mla_q_rank), lambda i, j, k: (0, 0)),
         pl.BlockSpec((1, dm.mla_kv_rank), lambda i, j, k: (0, 0)),
         _row_spec(tms, LANES), _row_spec(tms, LANES), _row_spec(tms, LANES)],
        [jax.ShapeDtypeStruct((n, dm.mla_q_rank), BF16), jax.ShapeDtypeStruct((n, dm.mla_kv_rank), BF16),
         jax.ShapeDtypeStruct((n, LANES), BF16)],
        [_row_spec(tms, dm.mla_q_rank), _row_spec(tms, dm.mla_kv_rank), _row_spec(tms, LANES)],
        functools.partial(_ep_mla_pre, q_rank=dm.mla_q_rank, kv_rank=dm.mla_kv_rank, eps=dm.eps),
        tm=tms, tn=mla_cols, name="proj_mla_latents")

    heads = dm.mla_heads
    hb = min(4, heads)
    s_tiles = dm.seq // tms

    def head_spec(width):
        return pl.BlockSpec((None, hb, tms, width), lambda i, j, k: (i // s_tiles, j, i % s_tiles, 0))

    scale = (dm.mla_nope + dm.mla_rope) ** -0.5 * LOG2_E
    (q_full,) = matmul(
        [cq], [lp["w_uq"]], [0], [c, s1, s2],
        [_row_spec(tms, LANES), _row_spec(tms, LANES), _row_spec(tms, LANES)],
        [jax.ShapeDtypeStruct((dm.batch, heads, dm.seq, 2 * LANES), BF16)], [head_spec(2 * LANES)],
        functools.partial(_ep_mla_q, heads=hb, scale=scale), tm=tms, tn=hb * 2 * LANES, name="mla_q_up")
    k_full, v = matmul(
        [ckv], [lp["w_ukv"]], [0], [kpe], [_row_spec(tms, LANES)],
        [jax.ShapeDtypeStruct((dm.batch, heads, dm.seq, 2 * LANES), BF16),
         jax.ShapeDtypeStruct((dm.batch, heads, dm.seq, dm.mla_dv), BF16)],
        [head_spec(2 * LANES), head_spec(dm.mla_dv)],
        functools.partial(_ep_mla_kv, heads=hb), tm=tms, tn=hb * 2 * LANES, name="mla_kv_up")
    o_b = attention(q_full, k_full, v, tq=dm.tq, chunk=dm.chunk).reshape(n, heads * dm.mla_dv)

    o_a = hgrn2(hg.reshape(dm.batch, dm.seq, hg_cols), lp["lb"], lp["hg_gain"], heads=dm.hg_heads, dk=dm.hg_dk,
                dv=dm.hg_dv, rows=dm.hgrn_rows, eps=dm.eps).reshape(n, dm.hg_heads * dm.hg_dv)

    (y,) = matmul([o_a, o_b], [lp["w_branch_a"], lp["w_branch_b"]], [0, 1], [gates, gates],
                  [_tile_spec(dm.tm, dm.tn2), _tile_spec(dm.tm, dm.tn2, d // dm.tn2)],
                  [jax.ShapeDtypeStruct((n, d), BF16)], [_tile_spec(dm.tm, dm.tn2)], _ep_gated_sum,
                  tm=dm.tm, tn=dm.tn2, name="branch_merge")
    (x,) = matmul([y], [lp["w_out"]], [0], [x], [_tile_spec(dm.tm, dm.tn2)],
                  [jax.ShapeDtypeStruct((n, d), F32)], [_tile_spec(dm.tm, dm.tn2)], _ep_residual,
                  tm=dm.tm, tn=dm.tn2, name="mixer_out")
    return x


def _dense_ffn(x, lp, dm):
    n, d = x.shape
    h = rmsnorm(x, lp["norm_ffn"], BF16, tm=dm.tm_norm, eps=dm.eps)
    (u,) = matmul([h], [lp["w1"], lp["w3"]], [0, 0], [], [], [jax.ShapeDtypeStruct((n, dm.d_ff_pad), BF16)],
                  [_tile_spec(dm.tm, dm.tn2)], _ep_swiglu, tm=dm.tm, tn=dm.tn2, name="ffn_up")
    (x,) = matmul([u], [lp["w2"]], [0], [x], [_tile_spec(dm.tm, dm.tn)], [jax.ShapeDtypeStruct((n, d), F32)],
                  [_tile_spec(dm.tm, dm.tn)], _ep_residual, tm=dm.tm, tn=dm.tn, tk=dm.tk_ffn, name="ffn_down")
    return x


def _moe_ffn(x, lp, dm):
    n, d = x.shape
    h, comb = router(x, lp["norm_ffn"], lp["w_router"], tm=dm.tm_norm, eps=dm.eps)
    for e in range(dm.n_experts):
        (u,) = matmul([h], [lp["w1"][e], lp["w3"][e]], [0, 0], [], [],
                      [jax.ShapeDtypeStruct((n, dm.d_ff_expert), BF16)], [_tile_spec(dm.tm, dm.tn2)], _ep_swiglu,
                      tm=dm.tm, tn=dm.tn2, name="moe_up")
        (x,) = matmul([u], [lp["w2"][e]], [0], [x, comb], [_tile_spec(dm.tm, dm.tn2), _row_spec(dm.tm, LANES)],
                      [jax.ShapeDtypeStruct((n, d), F32)], [_tile_spec(dm.tm, dm.tn2)],
                      functools.partial(_ep_expert_accumulate, expert=e), tm=dm.tm, tn=dm.tn2, name="moe_down")
    return x


def _prepare_layer(l, p, lbs, dm):
    d = dm.d_model
    hg_cols = 4 * dm.hg_heads * dm.hg_dk
    mla_in = dm.mla_q_rank + dm.mla_kv_rank + dm.mla_rope
    w_in = p["w_in"][l]
    w_mla = jnp.pad(w_in[:, hg_cols:hg_cols + mla_in], ((0, 0), (0, LANES - dm.mla_rope)))
    heads = dm.mla_heads
    w_uq = p["w_uq"][l].reshape(dm.mla_q_rank, heads, dm.mla_nope + dm.mla_rope)
    w_uq = jnp.pad(w_uq, ((0, 0), (0, 0), (0, 2 * LANES - dm.mla_nope - dm.mla_rope)))
    hg_width = dm.hg_heads * dm.hg_dv
    lp = {
        "norm_mix": p["norm_mix"][l],
        "w_hg": w_in[:, :hg_cols].astype(BF16),
        "w_mla": w_mla.astype(BF16),
        "w_gates": w_in[:, hg_cols + mla_in:].astype(BF16),
        "lb": lbs[l],
        "hg_gain": p["hg_norm"][l],
        "q_gain": p["mla_q_norm"][l].reshape(1, -1),
        "kv_gain": p["mla_kv_norm"][l].reshape(1, -1),
        "w_uq": w_uq.reshape(dm.mla_q_rank, heads * 2 * LANES).astype(BF16),
        "w_ukv": p["w_ukv"][l].astype(BF16),
        "w_branch_a": p["w_branch"][l, :hg_width].astype(BF16),
        "w_branch_b": p["w_branch"][l, hg_width:].astype(BF16),
        "w_out": p["w_out"][l].astype(BF16),
        "norm_ffn": p["norm_ffn"][l],
    }
    if l % 2 == 0:
        pad = dm.d_ff_pad - dm.d_ff
        lp["w1"] = jnp.pad(p["ffn_w1"][l // 2], ((0, 0), (0, pad))).astype(BF16)
        lp["w3"] = jnp.pad(p["ffn_w3"][l // 2], ((0, 0), (0, pad))).astype(BF16)
        lp["w2"] = jnp.pad(p["ffn_w2"][l // 2], ((0, pad), (0, 0))).astype(BF16)
    else:
        lp["w_router"] = p["w_router"][l // 2]
        lp["w1"] = p["moe_w1"][l // 2].astype(BF16)
        lp["w3"] = p["moe_w3"][l // 2].astype(BF16)
        lp["w2"] = p["moe_w2"][l // 2].astype(BF16)
    return lp


def forward(p, dm):
    n = dm.batch * dm.seq
    x = p["x"].reshape(n, dm.d_model)
    tables = _rope_tables(p["positions"], dm)
    lbs = jnp.cumsum(jax.nn.softmax(p["hg_lb_logits"].astype(F32), axis=0), axis=0)
    lbs = lbs - lbs[0:1]
    for l in range(dm.depth):
        lp = _prepare_layer(l, p, lbs, dm)
        x = _mixer(x, lp, tables, dm)
        x = _dense_ffn(x, lp, dm) if l % 2 == 0 else _moe_ffn(x, lp, dm)
    out = rmsnorm(x, p["norm_final"], F32, tm=dm.tm_norm, eps=dm.eps)
    return out.reshape(dm.batch, dm.seq, dm.d_model)


def kernel(x, positions, norm_mix, w_in, hg_lb_logits, hg_norm, mla_q_norm, w_uq, mla_kv_norm, w_ukv, w_branch, w_out, norm_ffn, ffn_w1, ffn_w3, ffn_w2, w_router, moe_w1, moe_w3, moe_w2, norm_final):
    p = dict(x=x, positions=positions, norm_mix=norm_mix, w_in=w_in, hg_lb_logits=hg_lb_logits, hg_norm=hg_norm,
             mla_q_norm=mla_q_norm, w_uq=w_uq, mla_kv_norm=mla_kv_norm, w_ukv=w_ukv, w_branch=w_branch, w_out=w_out,
             norm_ffn=norm_ffn, ffn_w1=ffn_w1, ffn_w3=ffn_w3, ffn_w2=ffn_w2, w_router=w_router, moe_w1=moe_w1,
             moe_w3=moe_w3, moe_w2=moe_w2, norm_final=norm_final)
    return forward(p, Dims())
```

```python
import functools
from typing import NamedTuple

import jax
import jax.numpy as jnp
from jax import lax
from jax.experimental import pallas as pl
from jax.experimental.pallas import tpu as pltpu

F32 = jnp.float32
BF16 = jnp.bfloat16

LANES = 128
V7X_VMEM_BYTES = 64 * 1024 * 1024
VMEM_LIMIT_BYTES = V7X_VMEM_BYTES - 8 * 1024 * 1024


class Dims(NamedTuple):
    d_model: int = 4096
    batch: int = 2
    seq: int = 8192
    depth: int = 2
    chunk: int = 64
    eps: float = 1e-6
    hg_dk: int = 128
    hg_heads: int = 16
    hg_dv: int = 128
    mla_dv: int = 128
    mla_heads: int = 16
    mla_nope: int = 128
    mla_rope: int = 64
    mla_q_rank: int = 768
    mla_kv_rank: int = 512
    rope_theta: float = 10000.0
    d_ff: int = 11008
    n_experts: int = 8
    d_ff_expert: int = 4096
    tm: int = 1024
    tn: int = 1024
    tn2: int = 512
    tk_ffn: int = 2816
    d_ff_pad: int = 11264
    tm_small: int = 512
    tm_norm: int = 256
    tm_expert: int = 512
    tm_combine: int = 128
    tq: int = 1024
    hgrn_rows: int = 512


LOG2_E = 1.4426950408889634
SUB = 16
HG_CHUNK = 64


def _cparams(semantics):
    return pltpu.CompilerParams(dimension_semantics=semantics, vmem_limit_bytes=VMEM_LIMIT_BYTES)


def _sigmoid(x):
    return 1.0 / (1.0 + jnp.exp(-x))


def _silu(x):
    return x * _sigmoid(x)


def _rmsnorm_kernel(x_ref, g_ref, o_ref, *, eps):
    x = x_ref[...]
    ms = jnp.mean(x * x, axis=-1, keepdims=True)
    o_ref[...] = (x * lax.rsqrt(ms + eps) * g_ref[...]).astype(o_ref.dtype)


def rmsnorm(x, g, out_dtype, *, tm, eps):
    n, d = x.shape
    return pl.pallas_call(
        functools.partial(_rmsnorm_kernel, eps=eps),
        grid=(n // tm,),
        in_specs=[pl.BlockSpec((tm, d), lambda i: (i, 0)), pl.BlockSpec((1, d), lambda i: (0, 0))],
        out_specs=pl.BlockSpec((tm, d), lambda i: (i, 0)),
        out_shape=jax.ShapeDtypeStruct((n, d), out_dtype),
        compiler_params=_cparams(("parallel",)),
        name="rmsnorm",
    )(x, g.reshape(1, d))


def _mm_kernel(*refs, n_a, w_src, n_extra, n_out, nk, epilogue):
    n_w = len(w_src)
    a_refs = refs[:n_a]
    w_refs = refs[n_a:n_a + n_w]
    e_refs = refs[n_a + n_w:n_a + n_w + n_extra]
    o_refs = refs[n_a + n_w + n_extra:n_a + n_w + n_extra + n_out]
    acc_refs = refs[n_a + n_w + n_extra + n_out:]
    parts = [jnp.dot(a_refs[s][...], w[...], preferred_element_type=F32) for w, s in zip(w_refs, w_src)]
    if nk == 1:
        epilogue(parts, e_refs, o_refs)
        return
    k = pl.program_id(2)

    @pl.when(k == 0)
    def _():
        for acc, p in zip(acc_refs, parts):
            acc[...] = p

    @pl.when(k > 0)
    def _():
        for acc, p in zip(acc_refs, parts):
            acc[...] += p

    @pl.when(k == nk - 1)
    def _():
        epilogue([acc[...] for acc in acc_refs], e_refs, o_refs)


def matmul(a_list, w_list, w_src, extras, extra_specs, out_shapes, out_specs, epilogue, *, tm, tn, tk=None, name):
    m, kdim = a_list[0].shape
    n = w_list[0].shape[1]
    tk = kdim if tk is None else tk
    nk = kdim // tk
    assert m % tm == 0 and n % tn == 0 and kdim % tk == 0
    in_specs = [pl.BlockSpec((tm, tk), lambda i, j, k: (i, k)) for _ in a_list]
    in_specs += [pl.BlockSpec((tk, tn), lambda i, j, k: (k, j)) for _ in w_list]
    in_specs += list(extra_specs)
    scratch = [pltpu.VMEM((tm, tn), F32) for _ in w_list] if nk > 1 else []
    return pl.pallas_call(
        functools.partial(_mm_kernel, n_a=len(a_list), w_src=tuple(w_src), n_extra=len(extras),
                          n_out=len(out_shapes), nk=nk, epilogue=epilogue),
        grid=(m // tm, n // tn, nk),
        in_specs=in_specs,
        out_specs=list(out_specs),
        out_shape=list(out_shapes),
        scratch_shapes=scratch,
        compiler_params=_cparams(("parallel", "parallel", "arbitrary")),
        name=name,
    )(*a_list, *w_list, *extras)


def _tile_spec(tm, tn, col_block_offset=0):
    return pl.BlockSpec((tm, tn), lambda i, j, k: (i, j + col_block_offset))


def _row_spec(tm, width):
    return pl.BlockSpec((tm, width), lambda i, j, k: (i, 0))


def _ep_store(parts, e_refs, o_refs):
    o_refs[0][...] = parts[0].astype(o_refs[0].dtype)


def _ep_residual(parts, e_refs, o_refs):
    o_refs[0][...] = e_refs[0][...] + parts[0]


def _ep_swiglu(parts, e_refs, o_refs):
    o_refs[0][...] = (_silu(parts[0]) * parts[1]).astype(o_refs[0].dtype)


def _ep_gated_sum(parts, e_refs, o_refs):
    ga = e_refs[0][...].astype(F32)
    gb = e_refs[1][...].astype(F32)
    o_refs[0][...] = (_sigmoid(ga) * parts[0] + _sigmoid(gb) * parts[1]).astype(o_refs[0].dtype)


def _rope128(g, c, s1, s2):
    return g * c + pltpu.roll(g, 32, 1) * s1 + pltpu.roll(g, LANES - 32, 1) * s2


def _ep_mla_pre(parts, e_refs, o_refs, *, q_rank, kv_rank, eps):
    acc = parts[0]
    qg, kvg, c, s1, s2 = (r[...] for r in e_refs)
    cq = acc[:, :q_rank]
    ckv = acc[:, q_rank:q_rank + kv_rank]
    kp = acc[:, q_rank + kv_rank:q_rank + kv_rank + LANES]
    cq = cq * lax.rsqrt(jnp.mean(cq * cq, axis=-1, keepdims=True) + eps) * qg
    ckv = ckv * lax.rsqrt(jnp.mean(ckv * ckv, axis=-1, keepdims=True) + eps) * kvg
    o_refs[0][...] = cq.astype(o_refs[0].dtype)
    o_refs[1][...] = ckv.astype(o_refs[1].dtype)
    o_refs[2][...] = _rope128(kp, c, s1, s2).astype(o_refs[2].dtype)


def _ep_mla_q(parts, e_refs, o_refs, *, heads, scale):
    acc = parts[0]
    c, s1, s2 = (r[...] for r in e_refs)
    for h in range(heads):
        base = 2 * LANES * h
        o_refs[0][h, :, :LANES] = (acc[:, base:base + LANES] * scale).astype(o_refs[0].dtype)
        pe = _rope128(acc[:, base + LANES:base + 2 * LANES], c, s1, s2)
        o_refs[0][h, :, LANES:] = (pe * scale).astype(o_refs[0].dtype)


def _ep_mla_kv(parts, e_refs, o_refs, *, heads):
    acc = parts[0]
    kpe = e_refs[0][...]
    for h in range(heads):
        base = 2 * LANES * h
        o_refs[0][h, :, :LANES] = acc[:, base:base + LANES].astype(o_refs[0].dtype)
        o_refs[0][h, :, LANES:] = kpe
        o_refs[1][h] = acc[:, base + LANES:base + 2 * LANES].astype(o_refs[1].dtype)


ATTN_SPLIT = 2


def _attn_kernel(q_ref, k_ref, v_ref, o_ref, *, tq, chunk):
    qi = pl.program_id(2)
    dv = v_ref.shape[-1]
    rows = tq // ATTN_SPLIT
    qs = [q_ref[pl.ds(i * rows, rows), :] for i in range(ATTN_SPLIT)]
    r_chunk = lax.broadcasted_iota(jnp.int32, (rows, rows), 0) // chunk
    c_chunk = lax.broadcasted_iota(jnp.int32, (rows, rows), 1) // chunk
    diag_mask = c_chunk <= r_chunk

    def update(q, k, v, carry, mask):
        m, l, acc = carry
        s = lax.dot_general(q, k, (((1,), (1,)), ((), ())), preferred_element_type=F32)
        if mask is not None:
            s = jnp.where(mask, s, -jnp.inf)
        m_new = jnp.maximum(m, jnp.max(s, axis=1, keepdims=True))
        p = jnp.exp2(s - m_new[:, :1])
        alpha = jnp.exp2(m - m_new)
        l = alpha * l + jnp.sum(p, axis=1, keepdims=True)
        acc = alpha[:, :dv] * acc + jnp.dot(p.astype(BF16), v, preferred_element_type=F32)
        return m_new, l, acc

    def full_block(kb, carry):
        ks = pl.multiple_of(kb * tq, tq)
        k = k_ref[pl.ds(ks, tq), :]
        v = v_ref[pl.ds(ks, tq), :]
        return tuple(update(qs[i], k, v, carry[i], None) for i in range(ATTN_SPLIT))

    init = tuple((jnp.full((rows, LANES), -jnp.inf, F32), jnp.zeros((rows, LANES), F32), jnp.zeros((rows, dv), F32))
                 for _ in range(ATTN_SPLIT))
    carry = list(lax.fori_loop(0, qi, full_block, init))
    for i in range(ATTN_SPLIT):
        for d in range(i + 1):
            ks = pl.multiple_of(qi * tq + d * rows, rows)
            k = k_ref[pl.ds(ks, rows), :]
            v = v_ref[pl.ds(ks, rows), :]
            carry[i] = update(qs[i], k, v, carry[i], diag_mask if d == i else None)
    for i in range(ATTN_SPLIT):
        _, l, acc = carry[i]
        o_ref[pl.ds(i * rows, rows), :] = (acc / l[:, :dv]).astype(o_ref.dtype)


def attention(q_full, k_full, v, *, tq, chunk):
    b, h, s, dq = q_full.shape
    dv = v.shape[-1]
    assert s % tq == 0 and (tq // ATTN_SPLIT) % chunk == 0 and dv == LANES
    return pl.pallas_call(
        functools.partial(_attn_kernel, tq=tq, chunk=chunk),
        grid=(b, h, s // tq),
        in_specs=[
            pl.BlockSpec((None, None, tq, dq), lambda bi, hi, qi: (bi, hi, qi, 0)),
            pl.BlockSpec((None, None, s, dq), lambda bi, hi, qi: (bi, hi, 0, 0)),
            pl.BlockSpec((None, None, s, dv), lambda bi, hi, qi: (bi, hi, 0, 0)),
        ],
        out_specs=pl.BlockSpec((None, tq, dv), lambda bi, hi, qi: (bi, qi, hi)),
        out_shape=jax.ShapeDtypeStruct((b, s, h * dv), BF16),
        compiler_params=_cparams(("parallel", "parallel", "arbitrary")),
        name="mla_attention",
    )(q_full, k_full, v)


def _split3(x):
    hi = x.astype(BF16)
    r = x - hi.astype(F32)
    mid = r.astype(BF16)
    lo = (r - mid.astype(F32)).astype(BF16)
    return hi, mid, lo


def _hgrn_kernel(q_ref, f_ref, i_ref, og_ref, lb_ref, gain_ref, o_ref, state_ref, *, rows, eps):
    @pl.when(pl.program_id(2) == 0)
    def _():
        state_ref[...] = jnp.zeros_like(state_ref)

    dk = q_ref.shape[-1]
    dv = i_ref.shape[-1]
    lb = lb_ref[...]
    log_lb = jnp.log(lb)
    log_1m_lb = jnp.log1p(-lb)
    one_m_lb = 1.0 - lb
    gain = gain_ref[...]
    r64 = lax.broadcasted_iota(jnp.int32, (HG_CHUNK, HG_CHUNK), 0)
    c64 = lax.broadcasted_iota(jnp.int32, (HG_CHUNK, HG_CHUNK), 1)
    tril = jnp.where((c64 <= r64) & (r64 // SUB == c64 // SUB), 1.0, 0.0).astype(BF16)
    ones_k = jnp.ones((dk, LANES), BF16)
    ones_v = jnp.full((dv, LANES), 1.0 / dv, F32).astype(BF16)
    row_id = lax.broadcasted_iota(jnp.int32, (SUB, dk), 0)
    groups = [slice(SUB * j, SUB * (j + 1)) for j in range(rows // SUB)]

    z = f_ref[...]
    y = log_1m_lb + (jnp.minimum(z, 0.0) - jnp.log1p(jnp.exp(-jnp.abs(z))))
    g = jnp.maximum(log_lb, y) + jnp.log1p(jnp.exp(-jnp.abs(log_lb - y)))
    kk = one_m_lb / (1.0 + jnp.exp(z))
    q = _silu(q_ref[...])
    v = i_ref[...]
    g1, g2, g3 = _split3(g)
    b = jnp.concatenate([
        jnp.dot(tril, g1[c:c + HG_CHUNK], preferred_element_type=F32)
        + jnp.dot(tril, g2[c:c + HG_CHUNK], preferred_element_type=F32)
        + jnp.dot(tril, g3[c:c + HG_CHUNK], preferred_element_type=F32) for c in range(0, rows, HG_CHUNK)], axis=0)

    scores = []
    for sl in groups:
        bj, qj, kj = b[sl], q[sl], kk[sl]
        terms = []
        for s in range(SUB):
            d = jnp.minimum(bj - bj[s:s + 1, :], 0.0)
            w = jnp.exp(d) * (qj * kj[s:s + 1, :])
            terms.append(jnp.where(row_id >= s, w, 0.0).astype(BF16))
        scores.append(jnp.dot(jnp.concatenate(terms, axis=0), ones_k, preferred_element_type=F32))

    incs, decs = [], []
    for sl in groups:
        bj = b[sl]
        b_last = bj[SUB - 1:SUB, :]
        k_dec = kk[sl] * jnp.exp(b_last - bj)
        incs.append(lax.dot_general(v[sl].astype(BF16), k_dec.astype(BF16), (((0,), (0,)), ((), ())),
                                    preferred_element_type=F32))
        decs.append(jnp.exp(b_last))

    st = state_ref[...]
    q_dec = (q * jnp.exp(b)).astype(BF16)
    o_inter = []
    for j, sl in enumerate(groups):
        o_inter.append(lax.dot_general(q_dec[sl], st.astype(BF16), (((1,), (1,)), ((), ())),
                                       preferred_element_type=F32))
        st = st * decs[j] + incs[j]
    state_ref[...] = st

    outs = []
    for j, sl in enumerate(groups):
        vj = v[sl]
        o = o_inter[j]
        for s in range(SUB):
            o = o + scores[j][SUB * s:SUB * (s + 1), :dv] * vj[s:s + 1, :]
        outs.append(o)
    o = jnp.concatenate(outs, axis=0)
    ms = jnp.dot((o * o).astype(BF16), ones_v, preferred_element_type=F32)[:, :dv]
    o = o * lax.rsqrt(ms + eps) * gain * _silu(og_ref[...])
    o_ref[...] = o.astype(o_ref.dtype)


def hgrn2(hg, lb, gain, *, heads, dk, dv, rows, eps):
    b, s, _ = hg.shape
    assert dk == LANES and dv == LANES and s % rows == 0 and rows % HG_CHUNK == 0

    def col(group):
        return pl.BlockSpec((None, rows, dk), lambda bi, hi, si: (bi, si, group * heads + hi))

    vec = pl.BlockSpec((1, dk), lambda bi, hi, si: (0, hi))
    return pl.pallas_call(
        functools.partial(_hgrn_kernel, rows=rows, eps=eps),
        grid=(b, heads, s // rows),
        in_specs=[col(0), col(1), col(2), col(3), vec, vec],
        out_specs=pl.BlockSpec((None, rows, dv), lambda bi, hi, si: (bi, si, hi)),
        out_shape=jax.ShapeDtypeStruct((b, s, heads * dv), BF16),
        scratch_shapes=[pltpu.VMEM((dv, dk), F32)],
        compiler_params=_cparams(("parallel", "parallel", "arbitrary")),
        name="hgrn2",
    )(hg, hg, hg, hg, lb.reshape(1, -1), gain.reshape(1, -1))


META_E1, META_E2, META_R1, META_R2, META_W1, META_W2 = range(6)
HI16 = 0xFFFF0000


def _pack_bf16_pair(lo, hi):
    lo_bits = lax.bitcast_convert_type(lo.astype(jnp.bfloat16).astype(F32), jnp.uint32) >> 16
    hi_bits = lax.bitcast_convert_type(hi.astype(jnp.bfloat16).astype(F32), jnp.uint32)
    return hi_bits | lo_bits


def _unpack_bf16_pair(word):
    lo = lax.bitcast_convert_type(word << 16, F32).astype(BF16)
    hi = lax.bitcast_convert_type(word & jnp.uint32(HI16), F32).astype(BF16)
    return lo, hi


def _router_kernel(x_ref, g_ref, wr_ref, hp_ref, meta_ref, cnt_ref, base_ref, *, n_experts, eps):
    @pl.when(pl.program_id(0) == 0)
    def _():
        base_ref[...] = jnp.zeros_like(base_ref)

    tm, d = x_ref.shape
    half = d // 2
    x = x_ref[...]
    h = x * lax.rsqrt(jnp.mean(x * x, axis=-1, keepdims=True) + eps) * g_ref[...]
    packed = _pack_bf16_pair(h[:, :half], h[:, half:])
    slab = half // LANES
    for a in range(slab):
        hp_ref[pl.ds(a, tm, stride=slab), :] = packed[:, a * LANES:(a + 1) * LANES]

    logits = jnp.dot(h, wr_ref[...], preferred_element_type=F32, precision=lax.Precision.HIGHEST)
    lane = lax.broadcasted_iota(jnp.int32, logits.shape, 1)
    logits = jnp.where(lane < n_experts, logits, -jnp.inf)
    m1 = jnp.max(logits, axis=1, keepdims=True)
    i1 = jnp.min(jnp.where(logits == m1, lane, LANES), axis=1, keepdims=True)
    rest = jnp.where(lane == i1, -jnp.inf, logits)
    m2 = jnp.max(rest, axis=1, keepdims=True)
    i2 = jnp.min(jnp.where(rest == m2, lane, LANES), axis=1, keepdims=True)
    e2 = jnp.exp(m2 - m1)
    w1 = 1.0 / (1.0 + e2)
    w2 = e2 / (1.0 + e2)

    sel1 = lane == i1
    sel2 = lane == i2
    sel = jnp.where(sel1 | sel2, 1.0, 0.0)
    r_id = lax.broadcasted_iota(jnp.int32, (tm, tm), 0)
    c_id = lax.broadcasted_iota(jnp.int32, (tm, tm), 1)
    before = jnp.where(c_id < r_id, 1.0, 0.0).astype(BF16)
    rank = base_ref[...] + jnp.dot(before, sel.astype(BF16), preferred_element_type=F32)
    r1 = jnp.sum(jnp.where(sel1, rank, 0.0), axis=1, keepdims=True)
    r2 = jnp.sum(jnp.where(sel2, rank, 0.0), axis=1, keepdims=True)
    base_ref[...] += jnp.sum(sel, axis=0, keepdims=True)
    cnt_ref[...] = jnp.broadcast_to(base_ref[...], cnt_ref.shape)

    record = jnp.zeros(logits.shape, F32)
    for lane_id, val in ((META_E1, i1.astype(F32)), (META_E2, i2.astype(F32)), (META_R1, r1), (META_R2, r2),
                         (META_W1, w1), (META_W2, w2)):
        record = jnp.where(lane == lane_id, val, record)
    meta_ref[...] = record


def router(x, g, w_router, *, tm, eps):
    n, d = x.shape
    n_experts = w_router.shape[1]
    slab = d // 2 // LANES
    wr = jnp.zeros((d, LANES), F32).at[:, :n_experts].set(w_router)
    return pl.pallas_call(
        functools.partial(_router_kernel, n_experts=n_experts, eps=eps),
        grid=(n // tm,),
        in_specs=[pl.BlockSpec((tm, d), lambda i: (i, 0)), pl.BlockSpec((1, d), lambda i: (0, 0)),
                  pl.BlockSpec((d, LANES), lambda i: (0, 0))],
        out_specs=[pl.BlockSpec((tm * slab, LANES), lambda i: (i, 0)), pl.BlockSpec((tm, LANES), lambda i: (i, 0)),
                   pl.BlockSpec((8, LANES), lambda i: (0, 0))],
        out_shape=[jax.ShapeDtypeStruct((n * slab, LANES), jnp.uint32), jax.ShapeDtypeStruct((n, LANES), F32),
                   jax.ShapeDtypeStruct((8, LANES), F32)],
        scratch_shapes=[pltpu.VMEM((1, LANES), F32)],
        compiler_params=_cparams(("arbitrary",)),
        name="ffn_norm_router",
    )(x, g.reshape(1, d), wr)


def _row_copy(src_ref, src_row, dst_ref, dst_row, slab, sem):
    return pltpu.make_async_copy(src_ref.at[pl.ds(src_row * slab, slab), :],
                                 dst_ref.at[pl.ds(dst_row * slab, slab), :], sem)


def _dispatch_kernel(dest_ref, h_ref, zeros_ref, xs_ref, sem, *, tokens, slab):
    del zeros_ref
    base = pl.program_id(0) * tokens

    def issue(t, carry):
        for s in range(2):
            _row_copy(h_ref, base + t, xs_ref, dest_ref[2 * (base + t) + s], slab, sem).start()
        return carry

    def drain(t, carry):
        for s in range(2):
            _row_copy(h_ref, 0, xs_ref, 0, slab, sem).wait()
        return carry

    lax.fori_loop(0, tokens, issue, 0)
    lax.fori_loop(0, tokens, drain, 0)


def dispatch(dest, hp, rows_padded, *, tokens, slab):
    n = hp.shape[0] // slab
    zeros = jnp.zeros((rows_padded * slab, LANES), hp.dtype)
    return pl.pallas_call(
        functools.partial(_dispatch_kernel, tokens=tokens, slab=slab),
        grid_spec=pltpu.PrefetchScalarGridSpec(
            num_scalar_prefetch=1, grid=(n // tokens,),
            in_specs=[pl.BlockSpec(memory_space=pl.ANY), pl.BlockSpec(memory_space=pl.ANY)],
            out_specs=pl.BlockSpec(memory_space=pl.ANY),
            scratch_shapes=[pltpu.SemaphoreType.DMA(())]),
        out_shape=jax.ShapeDtypeStruct(zeros.shape, zeros.dtype),
        input_output_aliases={2: 0},
        compiler_params=_cparams(("arbitrary",)),
        name="moe_dispatch",
    )(dest, hp, zeros)


def _expert_up_kernel(te_ref, nu_ref, x_ref, w1_ref, w3_ref, u_ref, xb_ref, *, slab):
    i, j = pl.program_id(0), pl.program_id(1)
    used = i < nu_ref[0]
    tm = xb_ref.shape[0]
    half = slab * LANES

    @pl.when(used & (j == 0))
    def _():
        for a in range(slab):
            lo, hi = _unpack_bf16_pair(x_ref[pl.ds(a, tm, stride=slab), :])
            xb_ref[:, a * LANES:(a + 1) * LANES] = lo
            xb_ref[:, half + a * LANES:half + (a + 1) * LANES] = hi

    @pl.when(used)
    def _():
        xb = xb_ref[...]
        a1 = jnp.dot(xb, w1_ref[...], preferred_element_type=F32)
        a3 = jnp.dot(xb, w3_ref[...], preferred_element_type=F32)
        u_ref[...] = (_silu(a1) * a3).astype(u_ref.dtype)

    @pl.when(jnp.logical_not(used))
    def _():
        u_ref[...] = jnp.zeros_like(u_ref)


def _expert_down_kernel(te_ref, nu_ref, u_ref, w2_ref, y_ref, *, slab):
    i, j = pl.program_id(0), pl.program_id(1)
    used = i < nu_ref[0]
    tm, tn = u_ref.shape[0], w2_ref.shape[1]
    chunks = tn // LANES

    @pl.when(used)
    def _():
        acc = jnp.dot(u_ref[...], w2_ref[...], preferred_element_type=F32)
        for c in range(chunks):
            y_ref[pl.ds(j * chunks + c, tm, stride=slab), :] = acc[:, c * LANES:(c + 1) * LANES]

    @pl.when(jnp.logical_not(used) & (j == 0))
    def _():
        y_ref[...] = jnp.zeros_like(y_ref)


def expert_ffn(tile_expert, n_used, xs, w1, w3, w2, *, tm, tn):
    n_exp, d, dff = w1.shape
    in_slab = d // 2 // LANES
    out_slab = d // LANES
    rows = xs.shape[0] // in_slab
    n_tiles = rows // tm

    def tile(i, nu):
        return jnp.minimum(i, nu[0] - 1)

    u = pl.pallas_call(
        functools.partial(_expert_up_kernel, slab=in_slab),
        grid_spec=pltpu.PrefetchScalarGridSpec(
            num_scalar_prefetch=2, grid=(n_tiles, dff // tn),
            in_specs=[pl.BlockSpec((tm * in_slab, LANES), lambda i, j, te, nu: (tile(i, nu), 0)),
                      pl.BlockSpec((None, d, tn), lambda i, j, te, nu: (te[i], 0, j)),
                      pl.BlockSpec((None, d, tn), lambda i, j, te, nu: (te[i], 0, j))],
            out_specs=pl.BlockSpec((tm, tn), lambda i, j, te, nu: (i, j)),
            scratch_shapes=[pltpu.VMEM((tm, d), BF16)]),
        out_shape=jax.ShapeDtypeStruct((rows, dff), BF16),
        compiler_params=_cparams(("arbitrary", "arbitrary")),
        name="moe_expert_up",
    )(tile_expert, n_used, xs, w1, w3)
    return pl.pallas_call(
        functools.partial(_expert_down_kernel, slab=out_slab),
        grid_spec=pltpu.PrefetchScalarGridSpec(
            num_scalar_prefetch=2, grid=(n_tiles, d // tn),
            in_specs=[pl.BlockSpec((tm, dff), lambda i, j, te, nu: (tile(i, nu), 0)),
                      pl.BlockSpec((None, dff, tn), lambda i, j, te, nu: (te[i], 0, j))],
            out_specs=pl.BlockSpec((tm * out_slab, LANES), lambda i, j, te, nu: (i, 0))),
        out_shape=jax.ShapeDtypeStruct((rows * out_slab, LANES), F32),
        compiler_params=_cparams(("arbitrary", "arbitrary")),
        name="moe_expert_down",
    )(tile_expert, n_used, u, w2)


def _combine_kernel(dest_ref, x_ref, meta_ref, g_ref, y_ref, o_ref, buf_ref, sem, *, tokens, slab, eps, final_norm):
    i = pl.program_id(0)
    slot = i % 2

    def fetch(step, into):
        def issue(t, carry):
            for s in range(2):
                pltpu.make_async_copy(y_ref.at[pl.ds(dest_ref[2 * (step * tokens + t) + s] * slab, slab), :],
                                      buf_ref.at[into, s, pl.ds(t * slab, slab), :], sem.at[into]).start()
            return carry

        lax.fori_loop(0, tokens, issue, 0)

    @pl.when(i == 0)
    def _():
        fetch(0, 0)

    @pl.when(i + 1 < pl.num_programs(0))
    def _():
        fetch(i + 1, 1 - slot)

    def drain(t, carry):
        for s in range(2):
            pltpu.make_async_copy(y_ref.at[pl.ds(0, slab), :], buf_ref.at[slot, s, pl.ds(0, slab), :],
                                  sem.at[slot]).wait()
        return carry

    lax.fori_loop(0, tokens, drain, 0)

    meta = meta_ref[...]
    w1 = meta[:, META_W1:META_W1 + 1]
    w2 = meta[:, META_W2:META_W2 + 1]
    w1 = jnp.broadcast_to(w1, (tokens, LANES))
    w2 = jnp.broadcast_to(w2, (tokens, LANES))
    sq = jnp.zeros((tokens, LANES), F32)
    for c in range(slab):
        cols = slice(c * LANES, (c + 1) * LANES)
        y1 = buf_ref[slot, 0, pl.ds(c, tokens, stride=slab), :]
        y2 = buf_ref[slot, 1, pl.ds(c, tokens, stride=slab), :]
        o = x_ref[:, cols] + w1 * y1 + w2 * y2
        o_ref[:, cols] = o
        sq = sq + o * o
    if final_norm:
        d = slab * LANES
        inv = jnp.broadcast_to(lax.rsqrt(jnp.sum(sq, axis=1, keepdims=True) / d + eps), (tokens, LANES))
        for c in range(slab):
            cols = slice(c * LANES, (c + 1) * LANES)
            o_ref[:, cols] = o_ref[:, cols] * inv * g_ref[:, cols]


def combine(dest, x, meta, gain, y, *, tokens, eps, final_norm):
    n, d = x.shape
    slab = d // LANES
    return pl.pallas_call(
        functools.partial(_combine_kernel, tokens=tokens, slab=slab, eps=eps, final_norm=final_norm),
        grid_spec=pltpu.PrefetchScalarGridSpec(
            num_scalar_prefetch=1, grid=(n // tokens,),
            in_specs=[pl.BlockSpec((tokens, d), lambda i, dest: (i, 0)),
                      pl.BlockSpec((tokens, LANES), lambda i, dest: (i, 0)),
                      pl.BlockSpec((1, d), lambda i, dest: (0, 0)),
                      pl.BlockSpec(memory_space=pl.ANY)],
            out_specs=pl.BlockSpec((tokens, d), lambda i, dest: (i, 0)),
            scratch_shapes=[pltpu.VMEM((2, 2, tokens * slab, LANES), F32), pltpu.SemaphoreType.DMA((2,))]),
        out_shape=jax.ShapeDtypeStruct((n, d), F32),
        compiler_params=_cparams(("arbitrary",)),
        name="moe_combine",
    )(dest, x, meta, gain.reshape(1, d), y)


def _rope_tables(positions, dm):
    half = dm.mla_rope // 2
    inv_freq = dm.rope_theta ** (-jnp.arange(half, dtype=F32) / half)
    ang = positions.astype(F32).reshape(-1, 1) * inv_freq
    cos, sin = jnp.cos(ang), jnp.sin(ang)
    zero = jnp.zeros_like(cos)
    c = jnp.concatenate([cos, cos, zero, zero], axis=-1)
    s1 = jnp.concatenate([zero, sin, zero, zero], axis=-1)
    s2 = jnp.concatenate([-sin, zero, zero, zero], axis=-1)
    return c, s1, s2


def _mixer(x, lp, tables, dm):
    n = dm.batch * dm.seq
    d = dm.d_model
    hg_cols = 4 * dm.hg_heads * dm.hg_dk
    c, s1, s2 = tables
    h = rmsnorm(x, lp["norm_mix"], BF16, tm=dm.tm_norm, eps=dm.eps)

    (hg,) = matmul([h], [lp["w_hg"]], [0], [], [], [jax.ShapeDtypeStruct((n, hg_cols), F32)],
                   [_tile_spec(dm.tm, dm.tn)], _ep_store, tm=dm.tm, tn=dm.tn, name="proj_hgrn")
    (gates,) = matmul([h], [lp["w_gates"]], [0], [], [], [jax.ShapeDtypeStruct((n, 2 * d), BF16)],
                      [_tile_spec(dm.tm, dm.tn)], _ep_store, tm=dm.tm, tn=dm.tn, name="proj_gates")

    mla_cols = lp["w_mla"].shape[1]
    tms = dm.tm_small
    cq, ckv, kpe = matmul(
        [h], [lp["w_mla"]], [0],
        [lp["q_gain"], lp["kv_gain"], c, s1, s2],
        [pl.BlockSpec((1, dm.mla_q_rank), lambda i, j, k: (0, 0)),
         pl.BlockSpec((1, dm.mla_kv_rank), lambda i, j, k: (0, 0)),
         _row_spec(tms, LANES), _row_spec(tms, LANES), _row_spec(tms, LANES)],
        [jax.ShapeDtypeStruct((n, dm.mla_q_rank), BF16), jax.ShapeDtypeStruct((n, dm.mla_kv_rank), BF16),
         jax.ShapeDtypeStruct((n, LANES), BF16)],
        [_row_spec(tms, dm.mla_q_rank), _row_spec(tms, dm.mla_kv_rank), _row_spec(tms, LANES)],
        functools.partial(_ep_mla_pre, q_rank=dm.mla_q_rank, kv_rank=dm.mla_kv_rank, eps=dm.eps),
        tm=tms, tn=mla_cols, name="proj_mla_latents")

    heads = dm.mla_heads
    hb = min(4, heads)
    s_tiles = dm.seq // tms

    def head_spec(width):
        return pl.BlockSpec((None, hb, tms, width), lambda i, j, k: (i // s_tiles, j, i % s_tiles, 0))

    scale = (dm.mla_nope + dm.mla_rope) ** -0.5 * LOG2_E
    (q_full,) = matmul(
        [cq], [lp["w_uq"]], [0], [c, s1, s2],
        [_row_spec(tms, LANES), _row_spec(tms, LANES), _row_spec(tms, LANES)],
        [jax.ShapeDtypeStruct((dm.batch, heads, dm.seq, 2 * LANES), BF16)], [head_spec(2 * LANES)],
        functools.partial(_ep_mla_q, heads=hb, scale=scale), tm=tms, tn=hb * 2 * LANES, name="mla_q_up")
    k_full, v = matmul(
        [ckv], [lp["w_ukv"]], [0], [kpe], [_row_spec(tms, LANES)],
        [jax.ShapeDtypeStruct((dm.batch, heads, dm.seq, 2 * LANES), BF16),
         jax.ShapeDtypeStruct((dm.batch, heads, dm.seq, dm.mla_dv), BF16)],
        [head_spec(2 * LANES), head_spec(dm.mla_dv)],
        functools.partial(_ep_mla_kv, heads=hb), tm=tms, tn=hb * 2 * LANES, name="mla_kv_up")
    o_b = attention(q_full, k_full, v, tq=dm.tq, chunk=dm.chunk).reshape(n, heads * dm.mla_dv)

    o_a = hgrn2(hg.reshape(dm.batch, dm.seq, hg_cols), lp["lb"], lp["hg_gain"], heads=dm.hg_heads, dk=dm.hg_dk,
                dv=dm.hg_dv, rows=dm.hgrn_rows, eps=dm.eps).reshape(n, dm.hg_heads * dm.hg_dv)

    (y,) = matmul([o_a, o_b], [lp["w_branch_a"], lp["w_branch_b"]], [0, 1], [gates, gates],
                  [_tile_spec(dm.tm, dm.tn2), _tile_spec(dm.tm, dm.tn2, d // dm.tn2)],
                  [jax.ShapeDtypeStruct((n, d), BF16)], [_tile_spec(dm.tm, dm.tn2)], _ep_gated_sum,
                  tm=dm.tm, tn=dm.tn2, name="branch_merge")
    (x,) = matmul([y], [lp["w_out"]], [0], [x], [_tile_spec(dm.tm, dm.tn2)],
                  [jax.ShapeDtypeStruct((n, d), F32)], [_tile_spec(dm.tm, dm.tn2)], _ep_residual,
                  tm=dm.tm, tn=dm.tn2, name="mixer_out")
    return x


def _dense_ffn(x, lp, dm):
    n, d = x.shape
    h = rmsnorm(x, lp["norm_ffn"], BF16, tm=dm.tm_norm, eps=dm.eps)
    (u,) = matmul([h], [lp["w1"], lp["w3"]], [0, 0], [], [], [jax.ShapeDtypeStruct((n, dm.d_ff_pad), BF16)],
                  [_tile_spec(dm.tm, dm.tn2)], _ep_swiglu, tm=dm.tm, tn=dm.tn2, name="ffn_up")
    (x,) = matmul([u], [lp["w2"]], [0], [x], [_tile_spec(dm.tm, dm.tn)], [jax.ShapeDtypeStruct((n, d), F32)],
                  [_tile_spec(dm.tm, dm.tn)], _ep_residual, tm=dm.tm, tn=dm.tn, tk=dm.tk_ffn, name="ffn_down")
    return x


def _moe_ffn(x, lp, dm, final_gain):
    n, d = x.shape
    n_exp, tme = dm.n_experts, dm.tm_expert
    hp, meta, cnt = router(x, lp["norm_ffn"], lp["w_router"], tm=dm.tm_norm, eps=dm.eps)
    counts = cnt[0, :n_exp].astype(jnp.int32)
    padded = (counts + tme - 1) // tme * tme
    ends = jnp.cumsum(padded)
    starts = ends - padded
    e12 = meta[:, META_E1:META_E2 + 1].astype(jnp.int32)
    r12 = meta[:, META_R1:META_R2 + 1].astype(jnp.int32)
    dest = (starts[e12] + r12).reshape(-1)
    n_tiles = (2 * n) // tme + n_exp
    n_used = (ends[-1] // tme).reshape(1)
    tile_row = jnp.minimum(jnp.arange(n_tiles, dtype=jnp.int32), n_used - 1) * tme
    tile_expert = jnp.minimum(jnp.sum(tile_row[:, None] >= ends[None, :], axis=1), n_exp - 1).astype(jnp.int32)

    in_slab = d // 2 // LANES
    xs = dispatch(dest, hp, n_tiles * tme, tokens=dm.tm_small, slab=in_slab)
    y = expert_ffn(tile_expert, n_used, xs, lp["w1"], lp["w3"], lp["w2"], tm=tme, tn=dm.tn2)
    gain = jnp.ones((d,), F32) if final_gain is None else final_gain
    return combine(dest, x, meta, gain, y, tokens=dm.tm_combine, eps=dm.eps, final_norm=final_gain is not None)


def _prepare_layer(l, p, lbs, dm):
    d = dm.d_model
    hg_cols = 4 * dm.hg_heads * dm.hg_dk
    mla_in = dm.mla_q_rank + dm.mla_kv_rank + dm.mla_rope
    w_in = p["w_in"][l]
    w_mla = jnp.pad(w_in[:, hg_cols:hg_cols + mla_in], ((0, 0), (0, LANES - dm.mla_rope)))
    heads = dm.mla_heads
    w_uq = p["w_uq"][l].reshape(dm.mla_q_rank, heads, dm.mla_nope + dm.mla_rope)
    w_uq = jnp.pad(w_uq, ((0, 0), (0, 0), (0, 2 * LANES - dm.mla_nope - dm.mla_rope)))
    hg_width = dm.hg_heads * dm.hg_dv
    lp = {
        "norm_mix": p["norm_mix"][l],
        "w_hg": w_in[:, :hg_cols].astype(BF16),
        "w_mla": w_mla.astype(BF16),
        "w_gates": w_in[:, hg_cols + mla_in:].astype(BF16),
        "lb": lbs[l],
        "hg_gain": p["hg_norm"][l],
        "q_gain": p["mla_q_norm"][l].reshape(1, -1),
        "kv_gain": p["mla_kv_norm"][l].reshape(1, -1),
        "w_uq": w_uq.reshape(dm.mla_q_rank, heads * 2 * LANES).astype(BF16),
        "w_ukv": p["w_ukv"][l].astype(BF16),
        "w_branch_a": p["w_branch"][l, :hg_width].astype(BF16),
        "w_branch_b": p["w_branch"][l, hg_width:].astype(BF16),
        "w_out": p["w_out"][l].astype(BF16),
        "norm_ffn": p["norm_ffn"][l],
    }
    if l % 2 == 0:
        pad = dm.d_ff_pad - dm.d_ff
        lp["w1"] = jnp.pad(p["ffn_w1"][l // 2], ((0, 0), (0, pad))).astype(BF16)
        lp["w3"] = jnp.pad(p["ffn_w3"][l // 2], ((0, 0), (0, pad))).astype(BF16)
        lp["w2"] = jnp.pad(p["ffn_w2"][l // 2], ((0, pad), (0, 0))).astype(BF16)
    else:
        lp["w_router"] = p["w_router"][l // 2]
        lp["w1"] = p["moe_w1"][l // 2].astype(BF16)
        lp["w3"] = p["moe_w3"][l // 2].astype(BF16)
        lp["w2"] = p["moe_w2"][l // 2].astype(BF16)
    return lp


def forward(p, dm):
    n = dm.batch * dm.seq
    x = p["x"].reshape(n, dm.d_model)
    tables = _rope_tables(p["positions"], dm)
    lbs = jnp.cumsum(jax.nn.softmax(p["hg_lb_logits"].astype(F32), axis=0), axis=0)
    lbs = lbs - lbs[0:1]
    for l in range(dm.depth):
        lp = _prepare_layer(l, p, lbs, dm)
        x = _mixer(x, lp, tables, dm)
        last = l == dm.depth - 1
        if l % 2 == 0:
            x = _dense_ffn(x, lp, dm)
            if last:
                x = rmsnorm(x, p["norm_final"], F32, tm=dm.tm_norm, eps=dm.eps)
        else:
            x = _moe_ffn(x, lp, dm, p["norm_final"] if last else None)
    return x.reshape(dm.batch, dm.seq, dm.d_model)


def kernel(x, positions, norm_mix, w_in, hg_lb_logits, hg_norm, mla_q_norm, w_uq, mla_kv_norm, w_ukv, w_branch, w_out, norm_ffn, ffn_w1, ffn_w3, ffn_w2, w_router, moe_w1, moe_w3, moe_w2, norm_final):
    p = dict(x=x, positions=positions, norm_mix=norm_mix, w_in=w_in, hg_lb_logits=hg_lb_logits, hg_norm=hg_norm,
             mla_q_norm=mla_q_norm, w_uq=w_uq, mla_kv_norm=mla_kv_norm, w_ukv=w_ukv, w_branch=w_branch, w_out=w_out,
             norm_ffn=norm_ffn, ffn_w1=ffn_w1, ffn_w3=ffn_w3, ffn_w2=ffn_w2, w_router=w_router, moe_w1=moe_w1,
             moe_w3=moe_w3, moe_w2=moe_w2, norm_final=norm_final)
    return forward(p, Dims())
```

```python
import functools
from typing import NamedTuple

import jax
import jax.numpy as jnp
from jax import lax
from jax.experimental import pallas as pl
from jax.experimental.pallas import tpu as pltpu

F32 = jnp.float32
BF16 = jnp.bfloat16

LANES = 128
V7X_VMEM_BYTES = 64 * 1024 * 1024
VMEM_LIMIT_BYTES = V7X_VMEM_BYTES - 8 * 1024 * 1024


class Dims(NamedTuple):
    d_model: int = 4096
    batch: int = 2
    seq: int = 8192
    depth: int = 2
    chunk: int = 64
    eps: float = 1e-6
    hg_dk: int = 128
    hg_heads: int = 16
    hg_dv: int = 128
    mla_dv: int = 128
    mla_heads: int = 16
    mla_nope: int = 128
    mla_rope: int = 64
    mla_q_rank: int = 768
    mla_kv_rank: int = 512
    rope_theta: float = 10000.0
    d_ff: int = 11008
    n_experts: int = 8
    d_ff_expert: int = 4096
    tm: int = 1024
    tn: int = 1024
    tn2: int = 512
    tk_ffn: int = 2816
    d_ff_pad: int = 11264
    tm_small: int = 512
    tm_norm: int = 256
    tm_expert: int = 512
    tm_combine: int = 128
    tq: int = 1024
    hgrn_rows: int = 512


LOG2_E = 1.4426950408889634
HG_CHUNK = 64


def _cparams(semantics):
    return pltpu.CompilerParams(dimension_semantics=semantics, vmem_limit_bytes=VMEM_LIMIT_BYTES)


def _sigmoid(x):
    return 1.0 / (1.0 + jnp.exp(-x))


def _silu(x):
    return x * _sigmoid(x)


def _rmsnorm_kernel(x_ref, g_ref, o_ref, *, eps):
    x = x_ref[...]
    ms = jnp.mean(x * x, axis=-1, keepdims=True)
    o_ref[...] = (x * lax.rsqrt(ms + eps) * g_ref[...]).astype(o_ref.dtype)


def rmsnorm(x, g, out_dtype, *, tm, eps):
    n, d = x.shape
    return pl.pallas_call(
        functools.partial(_rmsnorm_kernel, eps=eps),
        grid=(n // tm,),
        in_specs=[pl.BlockSpec((tm, d), lambda i: (i, 0)), pl.BlockSpec((1, d), lambda i: (0, 0))],
        out_specs=pl.BlockSpec((tm, d), lambda i: (i, 0)),
        out_shape=jax.ShapeDtypeStruct((n, d), out_dtype),
        compiler_params=_cparams(("parallel",)),
        name="rmsnorm",
    )(x, g.reshape(1, d))


def _mm_kernel(*refs, n_a, w_src, n_extra, n_out, nk, epilogue):
    n_w = len(w_src)
    a_refs = refs[:n_a]
    w_refs = refs[n_a:n_a + n_w]
    e_refs = refs[n_a + n_w:n_a + n_w + n_extra]
    o_refs = refs[n_a + n_w + n_extra:n_a + n_w + n_extra + n_out]
    acc_refs = refs[n_a + n_w + n_extra + n_out:]
    parts = [jnp.dot(a_refs[s][...], w[...], preferred_element_type=F32) for w, s in zip(w_refs, w_src)]
    if nk == 1:
        epilogue(parts, e_refs, o_refs)
        return
    k = pl.program_id(2)

    @pl.when(k == 0)
    def _():
        for acc, p in zip(acc_refs, parts):
            acc[...] = p

    @pl.when(k > 0)
    def _():
        for acc, p in zip(acc_refs, parts):
            acc[...] += p

    @pl.when(k == nk - 1)
    def _():
        epilogue([acc[...] for acc in acc_refs], e_refs, o_refs)


def matmul(a_list, w_list, w_src, extras, extra_specs, out_shapes, out_specs, epilogue, *, tm, tn, tk=None, name):
    m, kdim = a_list[0].shape
    n = w_list[0].shape[1]
    tk = kdim if tk is None else tk
    nk = kdim // tk
    assert m % tm == 0 and n % tn == 0 and kdim % tk == 0
    in_specs = [pl.BlockSpec((tm, tk), lambda i, j, k: (i, k)) for _ in a_list]
    in_specs += [pl.BlockSpec((tk, tn), lambda i, j, k: (k, j)) for _ in w_list]
    in_specs += list(extra_specs)
    scratch = [pltpu.VMEM((tm, tn), F32) for _ in w_list] if nk > 1 else []
    return pl.pallas_call(
        functools.partial(_mm_kernel, n_a=len(a_list), w_src=tuple(w_src), n_extra=len(extras),
                          n_out=len(out_shapes), nk=nk, epilogue=epilogue),
        grid=(m // tm, n // tn, nk),
        in_specs=in_specs,
        out_specs=list(out_specs),
        out_shape=list(out_shapes),
        scratch_shapes=scratch,
        compiler_params=_cparams(("parallel", "parallel", "arbitrary")),
        name=name,
    )(*a_list, *w_list, *extras)


def _tile_spec(tm, tn, col_block_offset=0):
    return pl.BlockSpec((tm, tn), lambda i, j, k: (i, j + col_block_offset))


def _row_spec(tm, width):
    return pl.BlockSpec((tm, width), lambda i, j, k: (i, 0))


def _ep_store(parts, e_refs, o_refs):
    o_refs[0][...] = parts[0].astype(o_refs[0].dtype)


def _ep_residual(parts, e_refs, o_refs):
    o_refs[0][...] = e_refs[0][...] + parts[0]


def _ep_swiglu(parts, e_refs, o_refs):
    o_refs[0][...] = (_silu(parts[0]) * parts[1]).astype(o_refs[0].dtype)


def _ep_gated_sum(parts, e_refs, o_refs):
    ga = e_refs[0][...].astype(F32)
    gb = e_refs[1][...].astype(F32)
    o_refs[0][...] = (_sigmoid(ga) * parts[0] + _sigmoid(gb) * parts[1]).astype(o_refs[0].dtype)


def _rope128(g, c, s1, s2):
    return g * c + pltpu.roll(g, 32, 1) * s1 + pltpu.roll(g, LANES - 32, 1) * s2


def _ep_mla_pre(parts, e_refs, o_refs, *, q_rank, kv_rank, eps):
    acc = parts[0]
    qg, kvg, c, s1, s2 = (r[...] for r in e_refs)
    cq = acc[:, :q_rank]
    ckv = acc[:, q_rank:q_rank + kv_rank]
    kp = acc[:, q_rank + kv_rank:q_rank + kv_rank + LANES]
    cq = cq * lax.rsqrt(jnp.mean(cq * cq, axis=-1, keepdims=True) + eps) * qg
    ckv = ckv * lax.rsqrt(jnp.mean(ckv * ckv, axis=-1, keepdims=True) + eps) * kvg
    o_refs[0][...] = cq.astype(o_refs[0].dtype)
    o_refs[1][...] = ckv.astype(o_refs[1].dtype)
    o_refs[2][...] = _rope128(kp, c, s1, s2).astype(o_refs[2].dtype)


def _ep_mla_q(parts, e_refs, o_refs, *, heads, scale):
    acc = parts[0]
    c, s1, s2 = (r[...] for r in e_refs)
    for h in range(heads):
        base = 2 * LANES * h
        o_refs[0][h, :, :LANES] = (acc[:, base:base + LANES] * scale).astype(o_refs[0].dtype)
        pe = _rope128(acc[:, base + LANES:base + 2 * LANES], c, s1, s2)
        o_refs[0][h, :, LANES:] = (pe * scale).astype(o_refs[0].dtype)


def _ep_mla_kv(parts, e_refs, o_refs, *, heads):
    acc = parts[0]
    kpe = e_refs[0][...]
    for h in range(heads):
        base = 2 * LANES * h
        o_refs[0][h, :, :LANES] = acc[:, base:base + LANES].astype(o_refs[0].dtype)
        o_refs[0][h, :, LANES:] = kpe
        o_refs[1][h] = acc[:, base + LANES:base + 2 * LANES].astype(o_refs[1].dtype)


ATTN_SPLIT = 2


def _attn_kernel(q_ref, k_ref, v_ref, o_ref, *, tq, chunk):
    qi = pl.program_id(2)
    dv = v_ref.shape[-1]
    rows = tq // ATTN_SPLIT
    qs = [q_ref[pl.ds(i * rows, rows), :] for i in range(ATTN_SPLIT)]
    r_chunk = lax.broadcasted_iota(jnp.int32, (rows, rows), 0) // chunk
    c_chunk = lax.broadcasted_iota(jnp.int32, (rows, rows), 1) // chunk
    diag_mask = c_chunk <= r_chunk

    def update(q, k, v, carry, mask):
        m, l, acc = carry
        s = lax.dot_general(q, k, (((1,), (1,)), ((), ())), preferred_element_type=F32)
        if mask is not None:
            s = jnp.where(mask, s, -jnp.inf)
        m_new = jnp.maximum(m, jnp.max(s, axis=1, keepdims=True))
        p = jnp.exp2(s - m_new[:, :1])
        alpha = jnp.exp2(m - m_new)
        l = alpha * l + jnp.sum(p, axis=1, keepdims=True)
        acc = alpha[:, :dv] * acc + jnp.dot(p.astype(BF16), v, preferred_element_type=F32)
        return m_new, l, acc

    def full_block(kb, carry):
        ks = pl.multiple_of(kb * tq, tq)
        k = k_ref[pl.ds(ks, tq), :]
        v = v_ref[pl.ds(ks, tq), :]
        return tuple(update(qs[i], k, v, carry[i], None) for i in range(ATTN_SPLIT))

    init = tuple((jnp.full((rows, LANES), -jnp.inf, F32), jnp.zeros((rows, LANES), F32), jnp.zeros((rows, dv), F32))
                 for _ in range(ATTN_SPLIT))
    carry = list(lax.fori_loop(0, qi, full_block, init))
    for i in range(ATTN_SPLIT):
        for d in range(i + 1):
            ks = pl.multiple_of(qi * tq + d * rows, rows)
            k = k_ref[pl.ds(ks, rows), :]
            v = v_ref[pl.ds(ks, rows), :]
            carry[i] = update(qs[i], k, v, carry[i], diag_mask if d == i else None)
    for i in range(ATTN_SPLIT):
        _, l, acc = carry[i]
        o_ref[pl.ds(i * rows, rows), :] = (acc / l[:, :dv]).astype(o_ref.dtype)


def attention(q_full, k_full, v, *, tq, chunk):
    b, h, s, dq = q_full.shape
    dv = v.shape[-1]
    assert s % tq == 0 and (tq // ATTN_SPLIT) % chunk == 0 and dv == LANES
    return pl.pallas_call(
        functools.partial(_attn_kernel, tq=tq, chunk=chunk),
        grid=(b, h, s // tq),
        in_specs=[
            pl.BlockSpec((None, None, tq, dq), lambda bi, hi, qi: (bi, hi, qi, 0)),
            pl.BlockSpec((None, None, s, dq), lambda bi, hi, qi: (bi, hi, 0, 0)),
            pl.BlockSpec((None, None, s, dv), lambda bi, hi, qi: (bi, hi, 0, 0)),
        ],
        out_specs=pl.BlockSpec((None, tq, dv), lambda bi, hi, qi: (bi, qi, hi)),
        out_shape=jax.ShapeDtypeStruct((b, s, h * dv), BF16),
        compiler_params=_cparams(("parallel", "parallel", "arbitrary")),
        name="mla_attention",
    )(q_full, k_full, v)


def _split3(x):
    hi = x.astype(BF16)
    r = x - hi.astype(F32)
    mid = r.astype(BF16)
    lo = (r - mid.astype(F32)).astype(BF16)
    return hi, mid, lo


def _hgrn_kernel(q_ref, f_ref, i_ref, og_ref, lb_ref, gain_ref, o_ref, state_ref, *, rows, eps):
    @pl.when(pl.program_id(2) == 0)
    def _():
        state_ref[...] = jnp.zeros_like(state_ref)

    dk = q_ref.shape[-1]
    dv = i_ref.shape[-1]
    lb = lb_ref[...]
    log_lb = jnp.log(lb)
    log_1m_lb = jnp.log1p(-lb)
    one_m_lb = 1.0 - lb
    gain = gain_ref[...]
    cl = HG_CHUNK
    halves = [1 << l for l in range(cl.bit_length() - 1)]
    chunks = [slice(c, c + cl) for c in range(0, rows, cl)]
    ones_k = jnp.ones((dk, LANES), BF16)
    ones_v = jnp.full((dv, LANES), 1.0 / dv, F32).astype(BF16)

    t_id = lax.broadcasted_iota(jnp.int32, (cl, cl), 0)
    s_id = lax.broadcasted_iota(jnp.int32, (cl, cl), 1)
    row = lax.broadcasted_iota(jnp.int32, (cl, dk), 0)
    cum_rows = [s_id <= t_id] + [s_id <= (t_id // (2 * h)) * (2 * h) + h for h in halves]
    cum_rows = jnp.concatenate([jnp.where(m, 1.0, 0.0) for m in cum_rows], axis=0).astype(BF16)
    upper = [(row % (2 * h)) >= h for h in halves]
    pair = [(t_id // (2 * h) == s_id // (2 * h)) & (t_id % (2 * h) >= h) & (s_id % (2 * h) < h) for h in halves]

    z = f_ref[...]
    y = log_1m_lb + (jnp.minimum(z, 0.0) - jnp.log1p(jnp.exp(-jnp.abs(z))))
    g = jnp.maximum(log_lb, y) + jnp.log1p(jnp.exp(-jnp.abs(log_lb - y)))
    kk = one_m_lb / (1.0 + jnp.exp(z))
    q = _silu(q_ref[...])
    v = i_ref[...]
    v16 = v.astype(BF16)
    g3 = jnp.concatenate(_split3(g), axis=1)

    cums = []
    for sl in chunks:
        c3 = jnp.dot(cum_rows, g3[sl], preferred_element_type=F32)
        cums.append(c3[:, :dk] + c3[:, dk:2 * dk] + c3[:, 2 * dk:])

    scores = []
    for sl, cum in zip(chunks, cums):
        b = cum[:cl]
        qc, kc = q[sl], kk[sl]
        acc = None
        for l, h in enumerate(halves):
            e = jnp.exp(-jnp.abs(b - cum[cl * (l + 1):cl * (l + 2)]))
            x = (jnp.where(upper[l], qc, kc) * e).astype(BF16)
            p = lax.dot_general(x, x, (((1,), (1,)), ((), ())), preferred_element_type=F32)
            p = jnp.where(pair[l], p, 0.0)
            acc = p if acc is None else acc + p
        scores.append(acc.astype(BF16))

    incs, decs, q_decs = [], [], []
    for sl, cum in zip(chunks, cums):
        b = cum[:cl]
        b_last = b[cl - 1:cl, :]
        k_dec = kk[sl] * jnp.exp(b_last - b)
        incs.append(lax.dot_general(v16[sl], k_dec.astype(BF16), (((0,), (0,)), ((), ())),
                                    preferred_element_type=F32))
        decs.append(jnp.exp(b_last))
        q_decs.append((q[sl] * jnp.exp(b)).astype(BF16))
    st = state_ref[...]
    o_inter = []
    for j in range(len(chunks)):
        o_inter.append(lax.dot_general(q_decs[j], st.astype(BF16), (((1,), (1,)), ((), ())),
                                       preferred_element_type=F32))
        st = st * decs[j] + incs[j]
    state_ref[...] = st

    outs = []
    for j, sl in enumerate(chunks):
        diag = jnp.dot((q[sl] * kk[sl]).astype(BF16), ones_k, preferred_element_type=F32)[:, :dv]
        outs.append(o_inter[j] + jnp.dot(scores[j], v16[sl], preferred_element_type=F32) + diag * v[sl])
    o = jnp.concatenate(outs, axis=0)
    ms = jnp.dot((o * o).astype(BF16), ones_v, preferred_element_type=F32)[:, :dv]
    o = o * lax.rsqrt(ms + eps) * gain * _silu(og_ref[...])
    o_ref[...] = o.astype(o_ref.dtype)


def hgrn2(hg, lb, gain, *, heads, dk, dv, rows, eps):
    b, s, _ = hg.shape
    assert dk == LANES and dv == LANES and s % rows == 0 and rows % HG_CHUNK == 0

    def col(group):
        return pl.BlockSpec((None, rows, dk), lambda bi, hi, si: (bi, si, group * heads + hi))

    vec = pl.BlockSpec((1, dk), lambda bi, hi, si: (0, hi))
    return pl.pallas_call(
        functools.partial(_hgrn_kernel, rows=rows, eps=eps),
        grid=(b, heads, s // rows),
        in_specs=[col(0), col(1), col(2), col(3), vec, vec],
        out_specs=pl.BlockSpec((None, rows, dv), lambda bi, hi, si: (bi, si, hi)),
        out_shape=jax.ShapeDtypeStruct((b, s, heads * dv), BF16),
        scratch_shapes=[pltpu.VMEM((dv, dk), F32)],
        compiler_params=_cparams(("parallel", "parallel", "arbitrary")),
        name="hgrn2",
    )(hg, hg, hg, hg, lb.reshape(1, -1), gain.reshape(1, -1))


META_E1, META_E2, META_R1, META_R2, META_W1, META_W2 = range(6)
HI16 = 0xFFFF0000


def _pack_bf16_pair(lo, hi):
    lo_bits = lax.bitcast_convert_type(lo.astype(jnp.bfloat16).astype(F32), jnp.uint32) >> 16
    hi_bits = lax.bitcast_convert_type(hi.astype(jnp.bfloat16).astype(F32), jnp.uint32)
    return hi_bits | lo_bits


def _unpack_bf16_pair(word):
    lo = lax.bitcast_convert_type(word << 16, F32).astype(BF16)
    hi = lax.bitcast_convert_type(word & jnp.uint32(HI16), F32).astype(BF16)
    return lo, hi


def _router_kernel(x_ref, g_ref, wr_ref, hp_ref, meta_ref, cnt_ref, base_ref, *, n_experts, eps):
    @pl.when(pl.program_id(0) == 0)
    def _():
        base_ref[...] = jnp.zeros_like(base_ref)

    tm, d = x_ref.shape
    half = d // 2
    x = x_ref[...]
    h = x * lax.rsqrt(jnp.mean(x * x, axis=-1, keepdims=True) + eps) * g_ref[...]
    packed = _pack_bf16_pair(h[:, :half], h[:, half:])
    slab = half // LANES
    for a in range(slab):
        hp_ref[pl.ds(a, tm, stride=slab), :] = packed[:, a * LANES:(a + 1) * LANES]

    logits = jnp.dot(h, wr_ref[...], preferred_element_type=F32, precision=lax.Precision.HIGHEST)
    lane = lax.broadcasted_iota(jnp.int32, logits.shape, 1)
    logits = jnp.where(lane < n_experts, logits, -jnp.inf)
    m1 = jnp.max(logits, axis=1, keepdims=True)
    i1 = jnp.min(jnp.where(logits == m1, lane, LANES), axis=1, keepdims=True)
    rest = jnp.where(lane == i1, -jnp.inf, logits)
    m2 = jnp.max(rest, axis=1, keepdims=True)
    i2 = jnp.min(jnp.where(rest == m2, lane, LANES), axis=1, keepdims=True)
    e2 = jnp.exp(m2 - m1)
    w1 = 1.0 / (1.0 + e2)
    w2 = e2 / (1.0 + e2)

    sel1 = lane == i1
    sel2 = lane == i2
    sel = jnp.where(sel1 | sel2, 1.0, 0.0)
    r_id = lax.broadcasted_iota(jnp.int32, (tm, tm), 0)
    c_id = lax.broadcasted_iota(jnp.int32, (tm, tm), 1)
    before = jnp.where(c_id < r_id, 1.0, 0.0).astype(BF16)
    rank = base_ref[...] + jnp.dot(before, sel.astype(BF16), preferred_element_type=F32)
    r1 = jnp.sum(jnp.where(sel1, rank, 0.0), axis=1, keepdims=True)
    r2 = jnp.sum(jnp.where(sel2, rank, 0.0), axis=1, keepdims=True)
    base_ref[...] += jnp.sum(sel, axis=0, keepdims=True)
    cnt_ref[...] = jnp.broadcast_to(base_ref[...], cnt_ref.shape)

    record = jnp.zeros(logits.shape, F32)
    for lane_id, val in ((META_E1, i1.astype(F32)), (META_E2, i2.astype(F32)), (META_R1, r1), (META_R2, r2),
                         (META_W1, w1), (META_W2, w2)):
        record = jnp.where(lane == lane_id, val, record)
    meta_ref[...] = record


def router(x, g, w_router, *, tm, eps):
    n, d = x.shape
    n_experts = w_router.shape[1]
    slab = d // 2 // LANES
    wr = jnp.zeros((d, LANES), F32).at[:, :n_experts].set(w_router)
    return pl.pallas_call(
        functools.partial(_router_kernel, n_experts=n_experts, eps=eps),
        grid=(n // tm,),
        in_specs=[pl.BlockSpec((tm, d), lambda i: (i, 0)), pl.BlockSpec((1, d), lambda i: (0, 0)),
                  pl.BlockSpec((d, LANES), lambda i: (0, 0))],
        out_specs=[pl.BlockSpec((tm * slab, LANES), lambda i: (i, 0)), pl.BlockSpec((tm, LANES), lambda i: (i, 0)),
                   pl.BlockSpec((8, LANES), lambda i: (0, 0))],
        out_shape=[jax.ShapeDtypeStruct((n * slab, LANES), jnp.uint32), jax.ShapeDtypeStruct((n, LANES), F32),
                   jax.ShapeDtypeStruct((8, LANES), F32)],
        scratch_shapes=[pltpu.VMEM((1, LANES), F32)],
        compiler_params=_cparams(("arbitrary",)),
        name="ffn_norm_router",
    )(x, g.reshape(1, d), wr)


def _row_copy(src_ref, src_row, dst_ref, dst_row, slab, sem):
    return pltpu.make_async_copy(src_ref.at[pl.ds(src_row * slab, slab), :],
                                 dst_ref.at[pl.ds(dst_row * slab, slab), :], sem)


def _dispatch_kernel(dest_ref, h_ref, zeros_ref, xs_ref, sem, *, tokens, slab):
    del zeros_ref
    base = pl.program_id(0) * tokens

    def issue(t, carry):
        for s in range(2):
            _row_copy(h_ref, t, xs_ref, dest_ref[2 * (base + t) + s], slab, sem).start()
        return carry

    def drain(t, carry):
        for s in range(2):
            _row_copy(h_ref, 0, xs_ref, 0, slab, sem).wait()
        return carry

    lax.fori_loop(0, tokens, issue, 0)
    lax.fori_loop(0, tokens, drain, 0)


def dispatch(dest, hp, rows_padded, *, tokens, slab):
    n = hp.shape[0] // slab
    zeros = jnp.zeros((rows_padded * slab, LANES), hp.dtype)
    return pl.pallas_call(
        functools.partial(_dispatch_kernel, tokens=tokens, slab=slab),
        grid_spec=pltpu.PrefetchScalarGridSpec(
            num_scalar_prefetch=1, grid=(n // tokens,),
            in_specs=[pl.BlockSpec((tokens * slab, LANES), lambda i, dest: (i, 0)),
                      pl.BlockSpec(memory_space=pl.ANY)],
            out_specs=pl.BlockSpec(memory_space=pl.ANY),
            scratch_shapes=[pltpu.SemaphoreType.DMA(())]),
        out_shape=jax.ShapeDtypeStruct(zeros.shape, zeros.dtype),
        input_output_aliases={2: 0},
        compiler_params=_cparams(("arbitrary",)),
        name="moe_dispatch",
    )(dest, hp, zeros)


def _expert_up_kernel(te_ref, nu_ref, x_ref, w1_ref, w3_ref, u_ref, xb_ref, *, slab):
    i, j = pl.program_id(0), pl.program_id(1)
    used = i < nu_ref[0]
    tm = xb_ref.shape[0]
    half = slab * LANES

    @pl.when(used & (j == 0))
    def _():
        for a in range(slab):
            lo, hi = _unpack_bf16_pair(x_ref[pl.ds(a, tm, stride=slab), :])
            xb_ref[:, a * LANES:(a + 1) * LANES] = lo
            xb_ref[:, half + a * LANES:half + (a + 1) * LANES] = hi

    @pl.when(used)
    def _():
        xb = xb_ref[...]
        a1 = jnp.dot(xb, w1_ref[...], preferred_element_type=F32)
        a3 = jnp.dot(xb, w3_ref[...], preferred_element_type=F32)
        u_ref[...] = (_silu(a1) * a3).astype(u_ref.dtype)

    @pl.when(jnp.logical_not(used))
    def _():
        u_ref[...] = jnp.zeros_like(u_ref)


def _expert_down_kernel(te_ref, nu_ref, u_ref, w2_ref, y_ref, *, slab):
    i, j = pl.program_id(0), pl.program_id(1)
    used = i < nu_ref[0]
    tm, tn = u_ref.shape[0], w2_ref.shape[1]
    chunks = tn // LANES

    @pl.when(used)
    def _():
        acc = jnp.dot(u_ref[...], w2_ref[...], preferred_element_type=F32)
        for c in range(chunks):
            y_ref[pl.ds(j * chunks + c, tm, stride=slab), :] = acc[:, c * LANES:(c + 1) * LANES]

    @pl.when(jnp.logical_not(used) & (j == 0))
    def _():
        y_ref[...] = jnp.zeros_like(y_ref)


def expert_ffn(tile_expert, n_used, xs, w1, w3, w2, *, tm, tn):
    n_exp, d, dff = w1.shape
    in_slab = d // 2 // LANES
    out_slab = d // LANES
    rows = xs.shape[0] // in_slab
    n_tiles = rows // tm

    def tile(i, nu):
        return jnp.minimum(i, nu[0] - 1)

    u = pl.pallas_call(
        functools.partial(_expert_up_kernel, slab=in_slab),
        grid_spec=pltpu.PrefetchScalarGridSpec(
            num_scalar_prefetch=2, grid=(n_tiles, dff // tn),
            in_specs=[pl.BlockSpec((tm * in_slab, LANES), lambda i, j, te, nu: (tile(i, nu), 0)),
                      pl.BlockSpec((None, d, tn), lambda i, j, te, nu: (te[i], 0, j)),
                      pl.BlockSpec((None, d, tn), lambda i, j, te, nu: (te[i], 0, j))],
            out_specs=pl.BlockSpec((tm, tn), lambda i, j, te, nu: (i, j)),
            scratch_shapes=[pltpu.VMEM((tm, d), BF16)]),
        out_shape=jax.ShapeDtypeStruct((rows, dff), BF16),
        compiler_params=_cparams(("arbitrary", "arbitrary")),
        name="moe_expert_up",
    )(tile_expert, n_used, xs, w1, w3)
    return pl.pallas_call(
        functools.partial(_expert_down_kernel, slab=out_slab),
        grid_spec=pltpu.PrefetchScalarGridSpec(
            num_scalar_prefetch=2, grid=(n_tiles, d // tn),
            in_specs=[pl.BlockSpec((tm, dff), lambda i, j, te, nu: (tile(i, nu), 0)),
                      pl.BlockSpec((None, dff, tn), lambda i, j, te, nu: (te[i], 0, j))],
            out_specs=pl.BlockSpec((tm * out_slab, LANES), lambda i, j, te, nu: (i, 0))),
        out_shape=jax.ShapeDtypeStruct((rows * out_slab, LANES), F32),
        compiler_params=_cparams(("arbitrary", "arbitrary")),
        name="moe_expert_down",
    )(tile_expert, n_used, u, w2)


def _combine_kernel(dest_ref, x_ref, meta_ref, g_ref, y_ref, o_ref, buf_ref, sem, *, tokens, slab, eps, final_norm):
    i = pl.program_id(0)
    slot = i % 2

    def fetch(step, into):
        def issue(t, carry):
            for s in range(2):
                pltpu.make_async_copy(y_ref.at[pl.ds(dest_ref[2 * (step * tokens + t) + s] * slab, slab), :],
                                      buf_ref.at[into, s, pl.ds(t * slab, slab), :], sem.at[into]).start()
            return carry

        lax.fori_loop(0, tokens, issue, 0)

    @pl.when(i == 0)
    def _():
        fetch(0, 0)

    @pl.when(i + 1 < pl.num_programs(0))
    def _():
        fetch(i + 1, 1 - slot)

    def drain(t, carry):
        for s in range(2):
            pltpu.make_async_copy(y_ref.at[pl.ds(0, slab), :], buf_ref.at[slot, s, pl.ds(0, slab), :],
                                  sem.at[slot]).wait()
        return carry

    lax.fori_loop(0, tokens, drain, 0)

    meta = meta_ref[...]
    w1 = meta[:, META_W1:META_W1 + 1]
    w2 = meta[:, META_W2:META_W2 + 1]
    w1 = jnp.broadcast_to(w1, (tokens, LANES))
    w2 = jnp.broadcast_to(w2, (tokens, LANES))
    sq = jnp.zeros((tokens, LANES), F32)
    for c in range(slab):
        cols = slice(c * LANES, (c + 1) * LANES)
        y1 = buf_ref[slot, 0, pl.ds(c, tokens, stride=slab), :]
        y2 = buf_ref[slot, 1, pl.ds(c, tokens, stride=slab), :]
        o = x_ref[:, cols] + w1 * y1 + w2 * y2
        o_ref[:, cols] = o
        sq = sq + o * o
    if final_norm:
        d = slab * LANES
        inv = jnp.broadcast_to(lax.rsqrt(jnp.sum(sq, axis=1, keepdims=True) / d + eps), (tokens, LANES))
        for c in range(slab):
            cols = slice(c * LANES, (c + 1) * LANES)
            o_ref[:, cols] = o_ref[:, cols] * inv * g_ref[:, cols]


def combine(dest, x, meta, gain, y, *, tokens, eps, final_norm):
    n, d = x.shape
    slab = d // LANES
    return pl.pallas_call(
        functools.partial(_combine_kernel, tokens=tokens, slab=slab, eps=eps, final_norm=final_norm),
        grid_spec=pltpu.PrefetchScalarGridSpec(
            num_scalar_prefetch=1, grid=(n // tokens,),
            in_specs=[pl.BlockSpec((tokens, d), lambda i, dest: (i, 0)),
                      pl.BlockSpec((tokens, LANES), lambda i, dest: (i, 0)),
                      pl.BlockSpec((1, d), lambda i, dest: (0, 0)),
                      pl.BlockSpec(memory_space=pl.ANY)],
            out_specs=pl.BlockSpec((tokens, d), lambda i, dest: (i, 0)),
            scratch_shapes=[pltpu.VMEM((2, 2, tokens * slab, LANES), F32), pltpu.SemaphoreType.DMA((2,))]),
        out_shape=jax.ShapeDtypeStruct((n, d), F32),
        compiler_params=_cparams(("arbitrary",)),
        name="moe_combine",
    )(dest, x, meta, gain.reshape(1, d), y)


def _rope_tables(positions, dm):
    half = dm.mla_rope // 2
    inv_freq = dm.rope_theta ** (-jnp.arange(half, dtype=F32) / half)
    ang = positions.astype(F32).reshape(-1, 1) * inv_freq
    cos, sin = jnp.cos(ang), jnp.sin(ang)
    zero = jnp.zeros_like(cos)
    c = jnp.concatenate([cos, cos, zero, zero], axis=-1)
    s1 = jnp.concatenate([zero, sin, zero, zero], axis=-1)
    s2 = jnp.concatenate([-sin, zero, zero, zero], axis=-1)
    return c, s1, s2


def _mixer(x, lp, tables, dm):
    n = dm.batch * dm.seq
    d = dm.d_model
    hg_cols = 4 * dm.hg_heads * dm.hg_dk
    c, s1, s2 = tables
    h = rmsnorm(x, lp["norm_mix"], BF16, tm=dm.tm_norm, eps=dm.eps)

    (hg,) = matmul([h], [lp["w_hg"]], [0], [], [], [jax.ShapeDtypeStruct((n, hg_cols), F32)],
                   [_tile_spec(dm.tm, dm.tn)], _ep_store, tm=dm.tm, tn=dm.tn, name="proj_hgrn")
    (gates,) = matmul([h], [lp["w_gates"]], [0], [], [], [jax.ShapeDtypeStruct((n, 2 * d), BF16)],
                      [_tile_spec(dm.tm, dm.tn)], _ep_store, tm=dm.tm, tn=dm.tn, name="proj_gates")

    mla_cols = lp["w_mla"].shape[1]
    tms = dm.tm_small
    cq, ckv, kpe = matmul(
        [h], [lp["w_mla"]], [0],
        [lp["q_gain"], lp["kv_gain"], c, s1, s2],
        [pl.BlockSpec((1, dm.mla_q_rank), lambda i, j, k: (0, 0)),
         pl.BlockSpec((1, dm.mla_kv_rank), lambda i, j, k: (0, 0)),
         _row_spec(tms, LANES), _row_spec(tms, LANES), _row_spec(tms, LANES)],
        [jax.ShapeDtypeStruct((n, dm.mla_q_rank), BF16), jax.ShapeDtypeStruct((n, dm.mla_kv_rank), BF16),
         jax.ShapeDtypeStruct((n, LANES), BF16)],
        [_row_spec(tms, dm.mla_q_rank), _row_spec(tms, dm.mla_kv_rank), _row_spec(tms, LANES)],
        functools.partial(_ep_mla_pre, q_rank=dm.mla_q_rank, kv_rank=dm.mla_kv_rank, eps=dm.eps),
        tm=tms, tn=mla_cols, name="proj_mla_latents")

    heads = dm.mla_heads
    hb = min(4, heads)
    s_tiles = dm.seq // tms

    def head_spec(width):
        return pl.BlockSpec((None, hb, tms, width), lambda i, j, k: (i // s_tiles, j, i % s_tiles, 0))

    scale = (dm.mla_nope + dm.mla_rope) ** -0.5 * LOG2_E
    (q_full,) = matmul(
        [cq], [lp["w_uq"]], [0], [c, s1, s2],
        [_row_spec(tms, LANES), _row_spec(tms, LANES), _row_spec(tms, LANES)],
        [jax.ShapeDtypeStruct((dm.batch, heads, dm.seq, 2 * LANES), BF16)], [head_spec(2 * LANES)],
        functools.partial(_ep_mla_q, heads=hb, scale=scale), tm=tms, tn=hb * 2 * LANES, name="mla_q_up")
    k_full, v = matmul(
        [ckv], [lp["w_ukv"]], [0], [kpe], [_row_spec(tms, LANES)],
        [jax.ShapeDtypeStruct((dm.batch, heads, dm.seq, 2 * LANES), BF16),
         jax.ShapeDtypeStruct((dm.batch, heads, dm.seq, dm.mla_dv), BF16)],
        [head_spec(2 * LANES), head_spec(dm.mla_dv)],
        functools.partial(_ep_mla_kv, heads=hb), tm=tms, tn=hb * 2 * LANES, name="mla_kv_up")
    o_b = attention(q_full, k_full, v, tq=dm.tq, chunk=dm.chunk).reshape(n, heads * dm.mla_dv)

    o_a = hgrn2(hg.reshape(dm.batch, dm.seq, hg_cols), lp["lb"], lp["hg_gain"], heads=dm.hg_heads, dk=dm.hg_dk,
                dv=dm.hg_dv, rows=dm.hgrn_rows, eps=dm.eps).reshape(n, dm.hg_heads * dm.hg_dv)

    (y,) = matmul([o_a, o_b], [lp["w_branch_a"], lp["w_branch_b"]], [0, 1], [gates, gates],
                  [_tile_spec(dm.tm, dm.tn2), _tile_spec(dm.tm, dm.tn2, d // dm.tn2)],
                  [jax.ShapeDtypeStruct((n, d), BF16)], [_tile_spec(dm.tm, dm.tn2)], _ep_gated_sum,
                  tm=dm.tm, tn=dm.tn2, name="branch_merge")
    (x,) = matmul([y], [lp["w_out"]], [0], [x], [_tile_spec(dm.tm, dm.tn2)],
                  [jax.ShapeDtypeStruct((n, d), F32)], [_tile_spec(dm.tm, dm.tn2)], _ep_residual,
                  tm=dm.tm, tn=dm.tn2, name="mixer_out")
    return x


def _dense_ffn(x, lp, dm):
    n, d = x.shape
    h = rmsnorm(x, lp["norm_ffn"], BF16, tm=dm.tm_norm, eps=dm.eps)
    (u,) = matmul([h], [lp["w1"], lp["w3"]], [0, 0], [], [], [jax.ShapeDtypeStruct((n, dm.d_ff_pad), BF16)],
                  [_tile_spec(dm.tm, dm.tn2)], _ep_swiglu, tm=dm.tm, tn=dm.tn2, name="ffn_up")
    (x,) = matmul([u], [lp["w2"]], [0], [x], [_tile_spec(dm.tm, dm.tn)], [jax.ShapeDtypeStruct((n, d), F32)],
                  [_tile_spec(dm.tm, dm.tn)], _ep_residual, tm=dm.tm, tn=dm.tn, tk=dm.tk_ffn, name="ffn_down")
    return x


def _moe_ffn(x, lp, dm, final_gain):
    n, d = x.shape
    n_exp, tme = dm.n_experts, dm.tm_expert
    hp, meta, cnt = router(x, lp["norm_ffn"], lp["w_router"], tm=dm.tm_norm, eps=dm.eps)
    counts = cnt[0, :n_exp].astype(jnp.int32)
    padded = (counts + tme - 1) // tme * tme
    ends = jnp.cumsum(padded)
    starts = ends - padded
    e12 = meta[:, META_E1:META_E2 + 1].astype(jnp.int32)
    r12 = meta[:, META_R1:META_R2 + 1].astype(jnp.int32)
    dest = (starts[e12] + r12).reshape(-1)
    n_tiles = (2 * n) // tme + n_exp
    n_used = (ends[-1] // tme).reshape(1)
    tile_row = jnp.minimum(jnp.arange(n_tiles, dtype=jnp.int32), n_used - 1) * tme
    tile_expert = jnp.minimum(jnp.sum(tile_row[:, None] >= ends[None, :], axis=1), n_exp - 1).astype(jnp.int32)

    in_slab = d // 2 // LANES
    xs = dispatch(dest, hp, n_tiles * tme, tokens=dm.tm_small, slab=in_slab)
    y = expert_ffn(tile_expert, n_used, xs, lp["w1"], lp["w3"], lp["w2"], tm=tme, tn=dm.tn2)
    gain = jnp.ones((d,), F32) if final_gain is None else final_gain
    return combine(dest, x, meta, gain, y, tokens=dm.tm_combine, eps=dm.eps, final_norm=final_gain is not None)


def _prepare_layer(l, p, lbs, dm):
    d = dm.d_model
    hg_cols = 4 * dm.hg_heads * dm.hg_dk
    mla_in = dm.mla_q_rank + dm.mla_kv_rank + dm.mla_rope
    w_in = p["w_in"][l]
    w_mla = jnp.pad(w_in[:, hg_cols:hg_cols + mla_in], ((0, 0), (0, LANES - dm.mla_rope)))
    heads = dm.mla_heads
    w_uq = p["w_uq"][l].reshape(dm.mla_q_rank, heads, dm.mla_nope + dm.mla_rope)
    w_uq = jnp.pad(w_uq, ((0, 0), (0, 0), (0, 2 * LANES - dm.mla_nope - dm.mla_rope)))
    hg_width = dm.hg_heads * dm.hg_dv
    lp = {
        "norm_mix": p["norm_mix"][l],
        "w_hg": w_in[:, :hg_cols].astype(BF16),
        "w_mla": w_mla.astype(BF16),
        "w_gates": w_in[:, hg_cols + mla_in:].astype(BF16),
        "lb": lbs[l],
        "hg_gain": p["hg_norm"][l],
        "q_gain": p["mla_q_norm"][l].reshape(1, -1),
        "kv_gain": p["mla_kv_norm"][l].reshape(1, -1),
        "w_uq": w_uq.reshape(dm.mla_q_rank, heads * 2 * LANES).astype(BF16),
        "w_ukv": p["w_ukv"][l].astype(BF16),
        "w_branch_a": p["w_branch"][l, :hg_width].astype(BF16),
        "w_branch_b": p["w_branch"][l, hg_width:].astype(BF16),
        "w_out": p["w_out"][l].astype(BF16),
        "norm_ffn": p["norm_ffn"][l],
    }
    if l % 2 == 0:
        pad = dm.d_ff_pad - dm.d_ff
        lp["w1"] = jnp.pad(p["ffn_w1"][l // 2], ((0, 0), (0, pad))).astype(BF16)
        lp["w3"] = jnp.pad(p["ffn_w3"][l // 2], ((0, 0), (0, pad))).astype(BF16)
        lp["w2"] = jnp.pad(p["ffn_w2"][l // 2], ((0, pad), (0, 0))).astype(BF16)
    else:
        lp["w_router"] = p["w_router"][l // 2]
        lp["w1"] = p["moe_w1"][l // 2].astype(BF16)
        lp["w3"] = p["moe_w3"][l // 2].astype(BF16)
        lp["w2"] = p["moe_w2"][l // 2].astype(BF16)
    return lp


def forward(p, dm):
    n = dm.batch * dm.seq
    x = p["x"].reshape(n, dm.d_model)
    tables = _rope_tables(p["positions"], dm)
    lbs = jnp.cumsum(jax.nn.softmax(p["hg_lb_logits"].astype(F32), axis=0), axis=0)
    lbs = lbs - lbs[0:1]
    for l in range(dm.depth):
        lp = _prepare_layer(l, p, lbs, dm)
        x = _mixer(x, lp, tables, dm)
        last = l == dm.depth - 1
        if l % 2 == 0:
            x = _dense_ffn(x, lp, dm)
            if last:
                x = rmsnorm(x, p["norm_final"], F32, tm=dm.tm_norm, eps=dm.eps)
        else:
            x = _moe_ffn(x, lp, dm, p["norm_final"] if last else None)
    return x.reshape(dm.batch, dm.seq, dm.d_model)


def kernel(x, positions, norm_mix, w_in, hg_lb_logits, hg_norm, mla_q_norm, w_uq, mla_kv_norm, w_ukv, w_branch, w_out, norm_ffn, ffn_w1, ffn_w3, ffn_w2, w_router, moe_w1, moe_w3, moe_w2, norm_final):
    p = dict(x=x, positions=positions, norm_mix=norm_mix, w_in=w_in, hg_lb_logits=hg_lb_logits, hg_norm=hg_norm,
             mla_q_norm=mla_q_norm, w_uq=w_uq, mla_kv_norm=mla_kv_norm, w_ukv=w_ukv, w_branch=w_branch, w_out=w_out,
             norm_ffn=norm_ffn, ffn_w1=ffn_w1, ffn_w3=ffn_w3, ffn_w2=ffn_w2, w_router=w_router, moe_w1=moe_w1,
             moe_w3=moe_w3, moe_w2=moe_w2, norm_final=norm_final)
    return forward(p, Dims())
```

```python
import functools
from typing import NamedTuple

import jax
import jax.numpy as jnp
from jax import lax
from jax.experimental import pallas as pl
from jax.experimental.pallas import tpu as pltpu

F32 = jnp.float32
BF16 = jnp.bfloat16

LANES = 128
V7X_VMEM_BYTES = 64 * 1024 * 1024
VMEM_LIMIT_BYTES = V7X_VMEM_BYTES - 8 * 1024 * 1024


class Dims(NamedTuple):
    d_model: int = 4096
    batch: int = 2
    seq: int = 8192
    depth: int = 2
    chunk: int = 64
    eps: float = 1e-6
    hg_dk: int = 128
    hg_heads: int = 16
    hg_dv: int = 128
    mla_dv: int = 128
    mla_heads: int = 16
    mla_nope: int = 128
    mla_rope: int = 64
    mla_q_rank: int = 768
    mla_kv_rank: int = 512
    rope_theta: float = 10000.0
    d_ff: int = 11008
    n_experts: int = 8
    d_ff_expert: int = 4096
    tm: int = 1024
    tn: int = 1024
    tn2: int = 512
    tk_ffn: int = 2816
    d_ff_pad: int = 11264
    tm_small: int = 512
    tm_norm: int = 256
    tm_expert: int = 512
    tm_combine: int = 256
    tm_up: int = 1024
    tq: int = 1024
    hgrn_rows: int = 1024


LOG2_E = 1.4426950408889634
HG_CHUNK = 64


def _cparams(semantics):
    return pltpu.CompilerParams(dimension_semantics=semantics, vmem_limit_bytes=VMEM_LIMIT_BYTES)


def _sigmoid(x):
    return 1.0 / (1.0 + jnp.exp(-x))


def _silu(x):
    return x * _sigmoid(x)


def _rmsnorm_kernel(x_ref, g_ref, o_ref, *, eps):
    x = x_ref[...]
    ms = jnp.mean(x * x, axis=-1, keepdims=True)
    o_ref[...] = (x * lax.rsqrt(ms + eps) * g_ref[...]).astype(o_ref.dtype)


def rmsnorm(x, g, out_dtype, *, tm, eps):
    n, d = x.shape
    return pl.pallas_call(
        functools.partial(_rmsnorm_kernel, eps=eps),
        grid=(n // tm,),
        in_specs=[pl.BlockSpec((tm, d), lambda i: (i, 0)), pl.BlockSpec((1, d), lambda i: (0, 0))],
        out_specs=pl.BlockSpec((tm, d), lambda i: (i, 0)),
        out_shape=jax.ShapeDtypeStruct((n, d), out_dtype),
        compiler_params=_cparams(("parallel",)),
        name="rmsnorm",
    )(x, g.reshape(1, d))


def _mm_kernel(*refs, n_a, w_src, n_extra, n_out, nk, epilogue):
    n_w = len(w_src)
    a_refs = refs[:n_a]
    w_refs = refs[n_a:n_a + n_w]
    e_refs = refs[n_a + n_w:n_a + n_w + n_extra]
    o_refs = refs[n_a + n_w + n_extra:n_a + n_w + n_extra + n_out]
    acc_refs = refs[n_a + n_w + n_extra + n_out:]
    parts = [jnp.dot(a_refs[s][...], w[...], preferred_element_type=F32) for w, s in zip(w_refs, w_src)]
    if nk == 1:
        epilogue(parts, e_refs, o_refs)
        return
    k = pl.program_id(2)

    @pl.when(k == 0)
    def _():
        for acc, p in zip(acc_refs, parts):
            acc[...] = p

    @pl.when(k > 0)
    def _():
        for acc, p in zip(acc_refs, parts):
            acc[...] += p

    @pl.when(k == nk - 1)
    def _():
        epilogue([acc[...] for acc in acc_refs], e_refs, o_refs)


def matmul(a_list, w_list, w_src, extras, extra_specs, out_shapes, out_specs, epilogue, *, tm, tn, tk=None, name):
    m, kdim = a_list[0].shape
    n = w_list[0].shape[1]
    tk = kdim if tk is None else tk
    nk = kdim // tk
    assert m % tm == 0 and n % tn == 0 and kdim % tk == 0
    in_specs = [pl.BlockSpec((tm, tk), lambda i, j, k: (i, k)) for _ in a_list]
    in_specs += [pl.BlockSpec((tk, tn), lambda i, j, k: (k, j)) for _ in w_list]
    in_specs += list(extra_specs)
    scratch = [pltpu.VMEM((tm, tn), F32) for _ in w_list] if nk > 1 else []
    return pl.pallas_call(
        functools.partial(_mm_kernel, n_a=len(a_list), w_src=tuple(w_src), n_extra=len(extras),
                          n_out=len(out_shapes), nk=nk, epilogue=epilogue),
        grid=(m // tm, n // tn, nk),
        in_specs=in_specs,
        out_specs=list(out_specs),
        out_shape=list(out_shapes),
        scratch_shapes=scratch,
        compiler_params=_cparams(("parallel", "parallel", "arbitrary")),
        name=name,
    )(*a_list, *w_list, *extras)


def _tile_spec(tm, tn, col_block_offset=0):
    return pl.BlockSpec((tm, tn), lambda i, j, k: (i, j + col_block_offset))


def _row_spec(tm, width):
    return pl.BlockSpec((tm, width), lambda i, j, k: (i, 0))


def _ep_store(parts, e_refs, o_refs):
    o_refs[0][...] = parts[0].astype(o_refs[0].dtype)


def _ep_residual(parts, e_refs, o_refs):
    o_refs[0][...] = e_refs[0][...] + parts[0]


def _ep_swiglu(parts, e_refs, o_refs):
    o_refs[0][...] = (_silu(parts[0]) * parts[1]).astype(o_refs[0].dtype)


def _ep_gated_sum(parts, e_refs, o_refs):
    ga = e_refs[0][...].astype(F32)
    gb = e_refs[1][...].astype(F32)
    o_refs[0][...] = (_sigmoid(ga) * parts[0] + _sigmoid(gb) * parts[1]).astype(o_refs[0].dtype)


def _rope128(g, c, s1, s2):
    return g * c + pltpu.roll(g, 32, 1) * s1 + pltpu.roll(g, LANES - 32, 1) * s2


def _ep_mla_pre(parts, e_refs, o_refs, *, q_rank, kv_rank, eps):
    acc = parts[0]
    qg, kvg, c, s1, s2 = (r[...] for r in e_refs)
    cq = acc[:, :q_rank]
    ckv = acc[:, q_rank:q_rank + kv_rank]
    kp = acc[:, q_rank + kv_rank:q_rank + kv_rank + LANES]
    cq = cq * lax.rsqrt(jnp.mean(cq * cq, axis=-1, keepdims=True) + eps) * qg
    ckv = ckv * lax.rsqrt(jnp.mean(ckv * ckv, axis=-1, keepdims=True) + eps) * kvg
    o_refs[0][...] = cq.astype(o_refs[0].dtype)
    o_refs[1][...] = ckv.astype(o_refs[1].dtype)
    o_refs[2][...] = _rope128(kp, c, s1, s2).astype(o_refs[2].dtype)


def _ep_mla_q(parts, e_refs, o_refs, *, heads, scale):
    acc = parts[0]
    c, s1, s2 = (r[...] for r in e_refs)
    for h in range(heads):
        base = 2 * LANES * h
        o_refs[0][h, :, :LANES] = (acc[:, base:base + LANES] * scale).astype(o_refs[0].dtype)
        pe = _rope128(acc[:, base + LANES:base + 2 * LANES], c, s1, s2)
        o_refs[0][h, :, LANES:] = (pe * scale).astype(o_refs[0].dtype)


def _ep_mla_kv(parts, e_refs, o_refs, *, heads):
    acc = parts[0]
    kpe = e_refs[0][...]
    for h in range(heads):
        base = 2 * LANES * h
        o_refs[0][h, :, :LANES] = acc[:, base:base + LANES].astype(o_refs[0].dtype)
        o_refs[0][h, :, LANES:] = kpe
        o_refs[1][h] = acc[:, base + LANES:base + 2 * LANES].astype(o_refs[1].dtype)


ATTN_SPLIT = 2


def _attn_kernel(q_ref, k_ref, v_ref, o_ref, *, tq, chunk):
    qi = pl.program_id(2)
    dv = v_ref.shape[-1]
    rows = tq // ATTN_SPLIT
    qs = [q_ref[pl.ds(i * rows, rows), :] for i in range(ATTN_SPLIT)]
    r_chunk = lax.broadcasted_iota(jnp.int32, (rows, rows), 0) // chunk
    c_chunk = lax.broadcasted_iota(jnp.int32, (rows, rows), 1) // chunk
    diag_mask = c_chunk <= r_chunk

    def update(q, k, v, carry, mask):
        m, l, acc = carry
        s = lax.dot_general(q, k, (((1,), (1,)), ((), ())), preferred_element_type=F32)
        if mask is not None:
            s = jnp.where(mask, s, -jnp.inf)
        m_new = jnp.maximum(m, jnp.max(s, axis=1, keepdims=True))
        p = jnp.exp2(s - m_new[:, :1])
        alpha = jnp.exp2(m - m_new)
        l = alpha * l + jnp.sum(p, axis=1, keepdims=True)
        acc = alpha[:, :dv] * acc + jnp.dot(p.astype(BF16), v, preferred_element_type=F32)
        return m_new, l, acc

    def full_block(kb, carry):
        ks = pl.multiple_of(kb * tq, tq)
        k = k_ref[pl.ds(ks, tq), :]
        v = v_ref[pl.ds(ks, tq), :]
        return tuple(update(qs[i], k, v, carry[i], None) for i in range(ATTN_SPLIT))

    init = tuple((jnp.full((rows, LANES), -jnp.inf, F32), jnp.zeros((rows, LANES), F32), jnp.zeros((rows, dv), F32))
                 for _ in range(ATTN_SPLIT))
    carry = list(lax.fori_loop(0, qi, full_block, init))
    for i in range(ATTN_SPLIT):
        for d in range(i + 1):
            ks = pl.multiple_of(qi * tq + d * rows, rows)
            k = k_ref[pl.ds(ks, rows), :]
            v = v_ref[pl.ds(ks, rows), :]
            carry[i] = update(qs[i], k, v, carry[i], diag_mask if d == i else None)
    for i in range(ATTN_SPLIT):
        _, l, acc = carry[i]
        o_ref[pl.ds(i * rows, rows), :] = (acc / l[:, :dv]).astype(o_ref.dtype)


def attention(q_full, k_full, v, *, tq, chunk):
    b, h, s, dq = q_full.shape
    dv = v.shape[-1]
    assert s % tq == 0 and (tq // ATTN_SPLIT) % chunk == 0 and dv == LANES
    return pl.pallas_call(
        functools.partial(_attn_kernel, tq=tq, chunk=chunk),
        grid=(b, h, s // tq),
        in_specs=[
            pl.BlockSpec((None, None, tq, dq), lambda bi, hi, qi: (bi, hi, qi, 0)),
            pl.BlockSpec((None, None, s, dq), lambda bi, hi, qi: (bi, hi, 0, 0)),
            pl.BlockSpec((None, None, s, dv), lambda bi, hi, qi: (bi, hi, 0, 0)),
        ],
        out_specs=pl.BlockSpec((None, tq, dv), lambda bi, hi, qi: (bi, qi, hi)),
        out_shape=jax.ShapeDtypeStruct((b, s, h * dv), BF16),
        compiler_params=_cparams(("parallel", "parallel", "arbitrary")),
        name="mla_attention",
    )(q_full, k_full, v)


def _split3(x):
    hi = x.astype(BF16)
    r = x - hi.astype(F32)
    mid = r.astype(BF16)
    lo = (r - mid.astype(F32)).astype(BF16)
    return hi, mid, lo


def _hgrn_kernel(q_ref, f_ref, i_ref, og_ref, lb_ref, gain_ref, o_ref, state_ref, *, rows, eps):
    @pl.when(pl.program_id(2) == 0)
    def _():
        state_ref[...] = jnp.zeros_like(state_ref)

    dk = q_ref.shape[-1]
    dv = i_ref.shape[-1]
    lb = lb_ref[...]
    log_lb = jnp.log(lb)
    log_1m_lb = jnp.log1p(-lb)
    one_m_lb = 1.0 - lb
    gain = gain_ref[...]
    cl = HG_CHUNK
    halves = [1 << l for l in range(cl.bit_length() - 1)]
    chunks = [slice(c, c + cl) for c in range(0, rows, cl)]
    ones_k = jnp.ones((dk, LANES), BF16)
    ones_v = jnp.full((dv, LANES), 1.0 / dv, F32).astype(BF16)

    t_id = lax.broadcasted_iota(jnp.int32, (cl, cl), 0)
    s_id = lax.broadcasted_iota(jnp.int32, (cl, cl), 1)
    row = lax.broadcasted_iota(jnp.int32, (cl, dk), 0)
    cum_rows = [s_id <= t_id] + [s_id <= (t_id // (2 * h)) * (2 * h) + h for h in halves]
    cum_rows = jnp.concatenate([jnp.where(m, 1.0, 0.0) for m in cum_rows], axis=0).astype(BF16)
    upper = [(row % (2 * h)) >= h for h in halves]
    pair = [(t_id // (2 * h) == s_id // (2 * h)) & (t_id % (2 * h) >= h) & (s_id % (2 * h) < h) for h in halves]

    z = f_ref[...]
    y = log_1m_lb + (jnp.minimum(z, 0.0) - jnp.log1p(jnp.exp(-jnp.abs(z))))
    g = jnp.maximum(log_lb, y) + jnp.log1p(jnp.exp(-jnp.abs(log_lb - y)))
    kk = one_m_lb / (1.0 + jnp.exp(z))
    q = _silu(q_ref[...])
    v = i_ref[...]
    v16 = v.astype(BF16)
    g3 = jnp.concatenate(_split3(g), axis=1)

    cums = []
    for sl in chunks:
        c3 = jnp.dot(cum_rows, g3[sl], preferred_element_type=F32)
        cums.append(c3[:, :dk] + c3[:, dk:2 * dk] + c3[:, 2 * dk:])

    scores = []
    for sl, cum in zip(chunks, cums):
        b = cum[:cl]
        qc, kc = q[sl], kk[sl]
        acc = None
        for l, h in enumerate(halves):
            e = jnp.exp(-jnp.abs(b - cum[cl * (l + 1):cl * (l + 2)]))
            x = (jnp.where(upper[l], qc, kc) * e).astype(BF16)
            p = lax.dot_general(x, x, (((1,), (1,)), ((), ())), preferred_element_type=F32)
            p = jnp.where(pair[l], p, 0.0)
            acc = p if acc is None else acc + p
        scores.append(acc.astype(BF16))

    incs, decs, q_decs = [], [], []
    for sl, cum in zip(chunks, cums):
        b = cum[:cl]
        b_last = b[cl - 1:cl, :]
        k_dec = kk[sl] * jnp.exp(b_last - b)
        incs.append(lax.dot_general(v16[sl], k_dec.astype(BF16), (((0,), (0,)), ((), ())),
                                    preferred_element_type=F32))
        decs.append(jnp.exp(b_last))
        q_decs.append((q[sl] * jnp.exp(b)).astype(BF16))
    st = state_ref[...]
    o_inter = []
    for j in range(len(chunks)):
        o_inter.append(lax.dot_general(q_decs[j], st.astype(BF16), (((1,), (1,)), ((), ())),
                                       preferred_element_type=F32))
        st = st * decs[j] + incs[j]
    state_ref[...] = st

    outs = []
    for j, sl in enumerate(chunks):
        diag = jnp.dot((q[sl] * kk[sl]).astype(BF16), ones_k, preferred_element_type=F32)[:, :dv]
        outs.append(o_inter[j] + jnp.dot(scores[j], v16[sl], preferred_element_type=F32) + diag * v[sl])
    o = jnp.concatenate(outs, axis=0)
    ms = jnp.dot((o * o).astype(BF16), ones_v, preferred_element_type=F32)[:, :dv]
    o = o * lax.rsqrt(ms + eps) * gain * _silu(og_ref[...])
    o_ref[...] = o.astype(o_ref.dtype)


def hgrn2(hg, lb, gain, *, heads, dk, dv, rows, eps):
    b, s, _ = hg.shape
    assert dk == LANES and dv == LANES and s % rows == 0 and rows % HG_CHUNK == 0

    def col(group):
        return pl.BlockSpec((None, rows, dk), lambda bi, hi, si: (bi, si, group * heads + hi))

    vec = pl.BlockSpec((1, dk), lambda bi, hi, si: (0, hi))
    return pl.pallas_call(
        functools.partial(_hgrn_kernel, rows=rows, eps=eps),
        grid=(b, heads, s // rows),
        in_specs=[col(0), col(1), col(2), col(3), vec, vec],
        out_specs=pl.BlockSpec((None, rows, dv), lambda bi, hi, si: (bi, si, hi)),
        out_shape=jax.ShapeDtypeStruct((b, s, heads * dv), BF16),
        scratch_shapes=[pltpu.VMEM((dv, dk), F32)],
        compiler_params=_cparams(("parallel", "parallel", "arbitrary")),
        name="hgrn2",
    )(hg, hg, hg, hg, lb.reshape(1, -1), gain.reshape(1, -1))


META_E1, META_E2, META_R1, META_R2, META_W1, META_W2 = range(6)
HI16 = 0xFFFF0000


def _pack_bf16_pair(lo, hi):
    lo_bits = lax.bitcast_convert_type(lo.astype(jnp.bfloat16).astype(F32), jnp.uint32) >> 16
    hi_bits = lax.bitcast_convert_type(hi.astype(jnp.bfloat16).astype(F32), jnp.uint32)
    return hi_bits | lo_bits


def _unpack_bf16_pair(word):
    lo = lax.bitcast_convert_type(word << 16, F32).astype(BF16)
    hi = lax.bitcast_convert_type(word & jnp.uint32(HI16), F32).astype(BF16)
    return lo, hi


def _router_kernel(x_ref, g_ref, wr_ref, hp_ref, meta_ref, cnt_ref, base_ref, *, n_experts, eps):
    @pl.when(pl.program_id(0) == 0)
    def _():
        base_ref[...] = jnp.zeros_like(base_ref)

    tm, d = x_ref.shape
    half = d // 2
    x = x_ref[...]
    h = x * lax.rsqrt(jnp.mean(x * x, axis=-1, keepdims=True) + eps) * g_ref[...]
    packed = _pack_bf16_pair(h[:, :half], h[:, half:])
    slab = half // LANES
    for a in range(slab):
        hp_ref[pl.ds(a, tm, stride=slab), :] = packed[:, a * LANES:(a + 1) * LANES]

    logits = jnp.dot(h, wr_ref[...], preferred_element_type=F32, precision=lax.Precision.HIGHEST)
    lane = lax.broadcasted_iota(jnp.int32, logits.shape, 1)
    logits = jnp.where(lane < n_experts, logits, -jnp.inf)
    m1 = jnp.max(logits, axis=1, keepdims=True)
    i1 = jnp.min(jnp.where(logits == m1, lane, LANES), axis=1, keepdims=True)
    rest = jnp.where(lane == i1, -jnp.inf, logits)
    m2 = jnp.max(rest, axis=1, keepdims=True)
    i2 = jnp.min(jnp.where(rest == m2, lane, LANES), axis=1, keepdims=True)
    e2 = jnp.exp(m2 - m1)
    w1 = 1.0 / (1.0 + e2)
    w2 = e2 / (1.0 + e2)

    sel1 = lane == i1
    sel2 = lane == i2
    sel = jnp.where(sel1 | sel2, 1.0, 0.0)
    r_id = lax.broadcasted_iota(jnp.int32, (tm, tm), 0)
    c_id = lax.broadcasted_iota(jnp.int32, (tm, tm), 1)
    before = jnp.where(c_id < r_id, 1.0, 0.0).astype(BF16)
    rank = base_ref[...] + jnp.dot(before, sel.astype(BF16), preferred_element_type=F32)
    r1 = jnp.sum(jnp.where(sel1, rank, 0.0), axis=1, keepdims=True)
    r2 = jnp.sum(jnp.where(sel2, rank, 0.0), axis=1, keepdims=True)
    base_ref[...] += jnp.sum(sel, axis=0, keepdims=True)
    cnt_ref[...] = jnp.broadcast_to(base_ref[...], cnt_ref.shape)

    record = jnp.zeros(logits.shape, F32)
    for lane_id, val in ((META_E1, i1.astype(F32)), (META_E2, i2.astype(F32)), (META_R1, r1), (META_R2, r2),
                         (META_W1, w1), (META_W2, w2)):
        record = jnp.where(lane == lane_id, val, record)
    meta_ref[...] = record


def router(x, g, w_router, *, tm, eps):
    n, d = x.shape
    n_experts = w_router.shape[1]
    slab = d // 2 // LANES
    wr = jnp.zeros((d, LANES), F32).at[:, :n_experts].set(w_router)
    return pl.pallas_call(
        functools.partial(_router_kernel, n_experts=n_experts, eps=eps),
        grid=(n // tm,),
        in_specs=[pl.BlockSpec((tm, d), lambda i: (i, 0)), pl.BlockSpec((1, d), lambda i: (0, 0)),
                  pl.BlockSpec((d, LANES), lambda i: (0, 0))],
        out_specs=[pl.BlockSpec((tm * slab, LANES), lambda i: (i, 0)), pl.BlockSpec((tm, LANES), lambda i: (i, 0)),
                   pl.BlockSpec((8, LANES), lambda i: (0, 0))],
        out_shape=[jax.ShapeDtypeStruct((n * slab, LANES), jnp.uint32), jax.ShapeDtypeStruct((n, LANES), F32),
                   jax.ShapeDtypeStruct((8, LANES), F32)],
        scratch_shapes=[pltpu.VMEM((1, LANES), F32)],
        compiler_params=_cparams(("arbitrary",)),
        name="ffn_norm_router",
    )(x, g.reshape(1, d), wr)


def _row_copy(src_ref, src_row, dst_ref, dst_row, slab, sem):
    return pltpu.make_async_copy(src_ref.at[pl.ds(src_row * slab, slab), :],
                                 dst_ref.at[pl.ds(dst_row * slab, slab), :], sem)


def _dispatch_kernel(dest_ref, h_ref, zeros_ref, xs_ref, sem, *, tokens, slab):
    del zeros_ref
    base = pl.program_id(0) * tokens

    def issue(t, carry):
        for s in range(2):
            _row_copy(h_ref, t, xs_ref, dest_ref[2 * (base + t) + s], slab, sem).start()
        return carry

    def drain(t, carry):
        for s in range(2):
            _row_copy(h_ref, 0, xs_ref, 0, slab, sem).wait()
        return carry

    lax.fori_loop(0, tokens, issue, 0)
    lax.fori_loop(0, tokens, drain, 0)


def dispatch(dest, hp, rows_padded, *, tokens, slab):
    n = hp.shape[0] // slab
    zeros = jnp.zeros((rows_padded * slab, LANES), hp.dtype)
    return pl.pallas_call(
        functools.partial(_dispatch_kernel, tokens=tokens, slab=slab),
        grid_spec=pltpu.PrefetchScalarGridSpec(
            num_scalar_prefetch=1, grid=(n // tokens,),
            in_specs=[pl.BlockSpec((tokens * slab, LANES), lambda i, dest: (i, 0)),
                      pl.BlockSpec(memory_space=pl.ANY)],
            out_specs=pl.BlockSpec(memory_space=pl.ANY),
            scratch_shapes=[pltpu.SemaphoreType.DMA(())]),
        out_shape=jax.ShapeDtypeStruct(zeros.shape, zeros.dtype),
        input_output_aliases={2: 0},
        compiler_params=_cparams(("arbitrary",)),
        name="moe_dispatch",
    )(dest, hp, zeros)


def _expert_up_kernel(te_ref, nu_ref, x_ref, w1_ref, w3_ref, u_ref, xb_ref, *, slab):
    i, j = pl.program_id(0), pl.program_id(1)
    used = i < nu_ref[0]
    tm = xb_ref.shape[0]
    half = slab * LANES

    @pl.when(used & (j == 0))
    def _():
        for a in range(slab):
            lo, hi = _unpack_bf16_pair(x_ref[pl.ds(a, tm, stride=slab), :])
            xb_ref[:, a * LANES:(a + 1) * LANES] = lo
            xb_ref[:, half + a * LANES:half + (a + 1) * LANES] = hi

    @pl.when(used)
    def _():
        xb = xb_ref[...]
        a1 = jnp.dot(xb, w1_ref[...], preferred_element_type=F32)
        a3 = jnp.dot(xb, w3_ref[...], preferred_element_type=F32)
        u_ref[...] = (_silu(a1) * a3).astype(u_ref.dtype)

    @pl.when(jnp.logical_not(used))
    def _():
        u_ref[...] = jnp.zeros_like(u_ref)


def _expert_down_kernel(te_ref, nu_ref, u_ref, w2_ref, y_ref, *, slab):
    i, j = pl.program_id(0), pl.program_id(1)
    used = i < nu_ref[0]
    tm, tn = u_ref.shape[0], w2_ref.shape[1]
    half = tn // 2
    chunks = half // LANES

    @pl.when(used)
    def _():
        acc = jnp.dot(u_ref[...], w2_ref[...], preferred_element_type=F32)
        packed = _pack_bf16_pair(acc[:, :half], acc[:, half:])
        for c in range(chunks):
            y_ref[pl.ds(j * chunks + c, tm, stride=slab), :] = packed[:, c * LANES:(c + 1) * LANES]

    @pl.when(jnp.logical_not(used) & (j == 0))
    def _():
        y_ref[...] = jnp.zeros_like(y_ref)


def expert_ffn(tile_expert, n_used, xs, w1, w3, w2, *, tm, tn):
    n_exp, d, dff = w1.shape
    in_slab = d // 2 // LANES
    out_slab = d // 2 // LANES
    rows = xs.shape[0] // in_slab
    n_tiles = rows // tm

    def tile(i, nu):
        return jnp.minimum(i, nu[0] - 1)

    u = pl.pallas_call(
        functools.partial(_expert_up_kernel, slab=in_slab),
        grid_spec=pltpu.PrefetchScalarGridSpec(
            num_scalar_prefetch=2, grid=(n_tiles, dff // tn),
            in_specs=[pl.BlockSpec((tm * in_slab, LANES), lambda i, j, te, nu: (tile(i, nu), 0)),
                      pl.BlockSpec((None, d, tn), lambda i, j, te, nu: (te[i], 0, j)),
                      pl.BlockSpec((None, d, tn), lambda i, j, te, nu: (te[i], 0, j))],
            out_specs=pl.BlockSpec((tm, tn), lambda i, j, te, nu: (i, j)),
            scratch_shapes=[pltpu.VMEM((tm, d), BF16)]),
        out_shape=jax.ShapeDtypeStruct((rows, dff), BF16),
        compiler_params=_cparams(("arbitrary", "arbitrary")),
        name="moe_expert_up",
    )(tile_expert, n_used, xs, w1, w3)
    return pl.pallas_call(
        functools.partial(_expert_down_kernel, slab=out_slab),
        grid_spec=pltpu.PrefetchScalarGridSpec(
            num_scalar_prefetch=2, grid=(n_tiles, d // tn),
            in_specs=[pl.BlockSpec((tm, dff), lambda i, j, te, nu: (tile(i, nu), 0)),
                      pl.BlockSpec((None, dff, tn), lambda i, j, te, nu: (te[i], 0, j))],
            out_specs=pl.BlockSpec((tm * out_slab, LANES), lambda i, j, te, nu: (i, 0))),
        out_shape=jax.ShapeDtypeStruct((rows * out_slab, LANES), jnp.uint32),
        compiler_params=_cparams(("arbitrary", "arbitrary")),
        name="moe_expert_down",
    )(tile_expert, n_used, u, w2)


def _combine_kernel(dest_ref, x_ref, meta_ref, g_ref, y_ref, o_ref, buf_ref, sem, *, tokens, slab, pair_block, eps,
                    final_norm):
    i = pl.program_id(0)
    slot = i % 2

    def fetch(step, into):
        def issue(t, carry):
            for s in range(2):
                pltpu.make_async_copy(y_ref.at[pl.ds(dest_ref[2 * (step * tokens + t) + s] * slab, slab), :],
                                      buf_ref.at[into, s, pl.ds(t * slab, slab), :], sem.at[into]).start()
            return carry

        lax.fori_loop(0, tokens, issue, 0)

    @pl.when(i == 0)
    def _():
        fetch(0, 0)

    @pl.when(i + 1 < pl.num_programs(0))
    def _():
        fetch(i + 1, 1 - slot)

    def drain(t, carry):
        for s in range(2):
            pltpu.make_async_copy(y_ref.at[pl.ds(0, slab), :], buf_ref.at[slot, s, pl.ds(0, slab), :],
                                  sem.at[slot]).wait()
        return carry

    lax.fori_loop(0, tokens, drain, 0)

    meta = meta_ref[...]
    w1 = meta[:, META_W1:META_W1 + 1]
    w2 = meta[:, META_W2:META_W2 + 1]
    w1 = jnp.broadcast_to(w1, (tokens, LANES))
    w2 = jnp.broadcast_to(w2, (tokens, LANES))
    sq = jnp.zeros((tokens, LANES), F32)
    half = pair_block // 2
    chunks = half // LANES
    for c in range(slab):
        lo_col = (c // chunks) * pair_block + (c % chunks) * LANES
        y1 = buf_ref[slot, 0, pl.ds(c, tokens, stride=slab), :]
        y2 = buf_ref[slot, 1, pl.ds(c, tokens, stride=slab), :]
        for col, shift in ((lo_col, True), (lo_col + half, False)):
            cols = slice(col, col + LANES)
            a1 = lax.bitcast_convert_type(y1 << 16 if shift else y1 & jnp.uint32(HI16), F32)
            a2 = lax.bitcast_convert_type(y2 << 16 if shift else y2 & jnp.uint32(HI16), F32)
            o = x_ref[:, cols] + w1 * a1 + w2 * a2
            o_ref[:, cols] = o
            sq = sq + o * o
    if final_norm:
        d = 2 * slab * LANES
        inv = jnp.broadcast_to(lax.rsqrt(jnp.sum(sq, axis=1, keepdims=True) / d + eps), (tokens, LANES))
        for c in range(2 * slab):
            cols = slice(c * LANES, (c + 1) * LANES)
            o_ref[:, cols] = o_ref[:, cols] * inv * g_ref[:, cols]


def combine(dest, x, meta, gain, y, *, tokens, pair_block, eps, final_norm):
    n, d = x.shape
    slab = d // 2 // LANES
    return pl.pallas_call(
        functools.partial(_combine_kernel, tokens=tokens, slab=slab, pair_block=pair_block, eps=eps,
                          final_norm=final_norm),
        grid_spec=pltpu.PrefetchScalarGridSpec(
            num_scalar_prefetch=1, grid=(n // tokens,),
            in_specs=[pl.BlockSpec((tokens, d), lambda i, dest: (i, 0)),
                      pl.BlockSpec((tokens, LANES), lambda i, dest: (i, 0)),
                      pl.BlockSpec((1, d), lambda i, dest: (0, 0)),
                      pl.BlockSpec(memory_space=pl.ANY)],
            out_specs=pl.BlockSpec((tokens, d), lambda i, dest: (i, 0)),
            scratch_shapes=[pltpu.VMEM((2, 2, tokens * slab, LANES), jnp.uint32), pltpu.SemaphoreType.DMA((2,))]),
        out_shape=jax.ShapeDtypeStruct((n, d), F32),
        compiler_params=_cparams(("arbitrary",)),
        name="moe_combine",
    )(dest, x, meta, gain.reshape(1, d), y)


def _rope_tables(positions, dm):
    half = dm.mla_rope // 2
    inv_freq = dm.rope_theta ** (-jnp.arange(half, dtype=F32) / half)
    ang = positions.astype(F32).reshape(-1, 1) * inv_freq
    cos, sin = jnp.cos(ang), jnp.sin(ang)
    zero = jnp.zeros_like(cos)
    c = jnp.concatenate([cos, cos, zero, zero], axis=-1)
    s1 = jnp.concatenate([zero, sin, zero, zero], axis=-1)
    s2 = jnp.concatenate([-sin, zero, zero, zero], axis=-1)
    return c, s1, s2


def _mixer(x, lp, tables, dm):
    n = dm.batch * dm.seq
    d = dm.d_model
    hg_cols = 4 * dm.hg_heads * dm.hg_dk
    c, s1, s2 = tables
    h = rmsnorm(x, lp["norm_mix"], BF16, tm=dm.tm_norm, eps=dm.eps)

    (hg,) = matmul([h], [lp["w_hg"]], [0], [], [], [jax.ShapeDtypeStruct((n, hg_cols), F32)],
                   [_tile_spec(dm.tm, dm.tn)], _ep_store, tm=dm.tm, tn=dm.tn, name="proj_hgrn")
    (gates,) = matmul([h], [lp["w_gates"]], [0], [], [], [jax.ShapeDtypeStruct((n, 2 * d), BF16)],
                      [_tile_spec(dm.tm, dm.tn)], _ep_store, tm=dm.tm, tn=dm.tn, name="proj_gates")

    mla_cols = lp["w_mla"].shape[1]
    tms = dm.tm_small
    cq, ckv, kpe = matmul(
        [h], [lp["w_mla"]], [0],
        [lp["q_gain"], lp["kv_gain"], c, s1, s2],
        [pl.BlockSpec((1, dm.mla_q_rank), lambda i, j, k: (0, 0)),
         pl.BlockSpec((1, dm.mla_kv_rank), lambda i, j, k: (0, 0)),
         _row_spec(tms, LANES), _row_spec(tms, LANES), _row_spec(tms, LANES)],
        [jax.ShapeDtypeStruct((n, dm.mla_q_rank), BF16), jax.ShapeDtypeStruct((n, dm.mla_kv_rank), BF16),
         jax.ShapeDtypeStruct((n, LANES), BF16)],
        [_row_spec(tms, dm.mla_q_rank), _row_spec(tms, dm.mla_kv_rank), _row_spec(tms, LANES)],
        functools.partial(_ep_mla_pre, q_rank=dm.mla_q_rank, kv_rank=dm.mla_kv_rank, eps=dm.eps),
        tm=tms, tn=mla_cols, name="proj_mla_latents")

    heads = dm.mla_heads
    hb = min(4, heads)
    tmu = dm.tm_up
    s_tiles = dm.seq // tmu

    def head_spec(width):
        return pl.BlockSpec((None, hb, tmu, width), lambda i, j, k: (i // s_tiles, j, i % s_tiles, 0))

    scale = (dm.mla_nope + dm.mla_rope) ** -0.5 * LOG2_E
    (q_full,) = matmul(
        [cq], [lp["w_uq"]], [0], [c, s1, s2],
        [_row_spec(tmu, LANES), _row_spec(tmu, LANES), _row_spec(tmu, LANES)],
        [jax.ShapeDtypeStruct((dm.batch, heads, dm.seq, 2 * LANES), BF16)], [head_spec(2 * LANES)],
        functools.partial(_ep_mla_q, heads=hb, scale=scale), tm=tmu, tn=hb * 2 * LANES, name="mla_q_up")
    k_full, v = matmul(
        [ckv], [lp["w_ukv"]], [0], [kpe], [_row_spec(tmu, LANES)],
        [jax.ShapeDtypeStruct((dm.batch, heads, dm.seq, 2 * LANES), BF16),
         jax.ShapeDtypeStruct((dm.batch, heads, dm.seq, dm.mla_dv), BF16)],
        [head_spec(2 * LANES), head_spec(dm.mla_dv)],
        functools.partial(_ep_mla_kv, heads=hb), tm=tmu, tn=hb * 2 * LANES, name="mla_kv_up")
    o_b = attention(q_full, k_full, v, tq=dm.tq, chunk=dm.chunk).reshape(n, heads * dm.mla_dv)

    o_a = hgrn2(hg.reshape(dm.batch, dm.seq, hg_cols), lp["lb"], lp["hg_gain"], heads=dm.hg_heads, dk=dm.hg_dk,
                dv=dm.hg_dv, rows=dm.hgrn_rows, eps=dm.eps).reshape(n, dm.hg_heads * dm.hg_dv)

    (y,) = matmul([o_a, o_b], [lp["w_branch_a"], lp["w_branch_b"]], [0, 1], [gates, gates],
                  [_tile_spec(dm.tm, dm.tn2), _tile_spec(dm.tm, dm.tn2, d // dm.tn2)],
                  [jax.ShapeDtypeStruct((n, d), BF16)], [_tile_spec(dm.tm, dm.tn2)], _ep_gated_sum,
                  tm=dm.tm, tn=dm.tn2, name="branch_merge")
    (x,) = matmul([y], [lp["w_out"]], [0], [x], [_tile_spec(dm.tm, dm.tn2)],
                  [jax.ShapeDtypeStruct((n, d), F32)], [_tile_spec(dm.tm, dm.tn2)], _ep_residual,
                  tm=dm.tm, tn=dm.tn2, name="mixer_out")
    return x


def _dense_ffn(x, lp, dm):
    n, d = x.shape
    h = rmsnorm(x, lp["norm_ffn"], BF16, tm=dm.tm_norm, eps=dm.eps)
    (u,) = matmul([h], [lp["w1"], lp["w3"]], [0, 0], [], [], [jax.ShapeDtypeStruct((n, dm.d_ff_pad), BF16)],
                  [_tile_spec(dm.tm, dm.tn2)], _ep_swiglu, tm=dm.tm, tn=dm.tn2, name="ffn_up")
    (x,) = matmul([u], [lp["w2"]], [0], [x], [_tile_spec(dm.tm, dm.tn)], [jax.ShapeDtypeStruct((n, d), F32)],
                  [_tile_spec(dm.tm, dm.tn)], _ep_residual, tm=dm.tm, tn=dm.tn, tk=dm.tk_ffn, name="ffn_down")
    return x


def _moe_ffn(x, lp, dm, final_gain):
    n, d = x.shape
    n_exp, tme = dm.n_experts, dm.tm_expert
    hp, meta, cnt = router(x, lp["norm_ffn"], lp["w_router"], tm=dm.tm_norm, eps=dm.eps)
    counts = cnt[0, :n_exp].astype(jnp.int32)
    padded = (counts + tme - 1) // tme * tme
    ends = jnp.cumsum(padded)
    starts = ends - padded
    e12 = meta[:, META_E1:META_E2 + 1].astype(jnp.int32)
    r12 = meta[:, META_R1:META_R2 + 1].astype(jnp.int32)
    dest = (starts[e12] + r12).reshape(-1)
    n_tiles = (2 * n) // tme + n_exp
    n_used = (ends[-1] // tme).reshape(1)
    tile_row = jnp.minimum(jnp.arange(n_tiles, dtype=jnp.int32), n_used - 1) * tme
    tile_expert = jnp.minimum(jnp.sum(tile_row[:, None] >= ends[None, :], axis=1), n_exp - 1).astype(jnp.int32)

    in_slab = d // 2 // LANES
    xs = dispatch(dest, hp, n_tiles * tme, tokens=dm.tm_small, slab=in_slab)
    y = expert_ffn(tile_expert, n_used, xs, lp["w1"], lp["w3"], lp["w2"], tm=tme, tn=dm.tn2)
    gain = jnp.ones((d,), F32) if final_gain is None else final_gain
    return combine(dest, x, meta, gain, y, tokens=dm.tm_combine, pair_block=dm.tn2, eps=dm.eps,
                   final_norm=final_gain is not None)


def _prepare_layer(l, p, lbs, dm):
    d = dm.d_model
    hg_cols = 4 * dm.hg_heads * dm.hg_dk
    mla_in = dm.mla_q_rank + dm.mla_kv_rank + dm.mla_rope
    w_in = p["w_in"][l]
    w_mla = jnp.pad(w_in[:, hg_cols:hg_cols + mla_in], ((0, 0), (0, LANES - dm.mla_rope)))
    heads = dm.mla_heads
    w_uq = p["w_uq"][l].reshape(dm.mla_q_rank, heads, dm.mla_nope + dm.mla_rope)
    w_uq = jnp.pad(w_uq, ((0, 0), (0, 0), (0, 2 * LANES - dm.mla_nope - dm.mla_rope)))
    hg_width = dm.hg_heads * dm.hg_dv
    lp = {
        "norm_mix": p["norm_mix"][l],
        "w_hg": w_in[:, :hg_cols].astype(BF16),
        "w_mla": w_mla.astype(BF16),
        "w_gates": w_in[:, hg_cols + mla_in:].astype(BF16),
        "lb": lbs[l],
        "hg_gain": p["hg_norm"][l],
        "q_gain": p["mla_q_norm"][l].reshape(1, -1),
        "kv_gain": p["mla_kv_norm"][l].reshape(1, -1),
        "w_uq": w_uq.reshape(dm.mla_q_rank, heads * 2 * LANES).astype(BF16),
        "w_ukv": p["w_ukv"][l].astype(BF16),
        "w_branch_a": p["w_branch"][l, :hg_width].astype(BF16),
        "w_branch_b": p["w_branch"][l, hg_width:].astype(BF16),
        "w_out": p["w_out"][l].astype(BF16),
        "norm_ffn": p["norm_ffn"][l],
    }
    if l % 2 == 0:
        pad = dm.d_ff_pad - dm.d_ff
        lp["w1"] = jnp.pad(p["ffn_w1"][l // 2], ((0, 0), (0, pad))).astype(BF16)
        lp["w3"] = jnp.pad(p["ffn_w3"][l // 2], ((0, 0), (0, pad))).astype(BF16)
        lp["w2"] = jnp.pad(p["ffn_w2"][l // 2], ((0, pad), (0, 0))).astype(BF16)
    else:
        lp["w_router"] = p["w_router"][l // 2]
        lp["w1"] = p["moe_w1"][l // 2].astype(BF16)
        lp["w3"] = p["moe_w3"][l // 2].astype(BF16)
        lp["w2"] = p["moe_w2"][l // 2].astype(BF16)
    return lp


def forward(p, dm):
    n = dm.batch * dm.seq
    x = p["x"].reshape(n, dm.d_model)
    tables = _rope_tables(p["positions"], dm)
    lbs = jnp.cumsum(jax.nn.softmax(p["hg_lb_logits"].astype(F32), axis=0), axis=0)
    lbs = lbs - lbs[0:1]
    for l in range(dm.depth):
        lp = _prepare_layer(l, p, lbs, dm)
        x = _mixer(x, lp, tables, dm)
        last = l == dm.depth - 1
        if l % 2 == 0:
            x = _dense_ffn(x, lp, dm)
            if last:
                x = rmsnorm(x, p["norm_final"], F32, tm=dm.tm_norm, eps=dm.eps)
        else:
            x = _moe_ffn(x, lp, dm, p["norm_final"] if last else None)
    return x.reshape(dm.batch, dm.seq, dm.d_model)


def kernel(x, positions, norm_mix, w_in, hg_lb_logits, hg_norm, mla_q_norm, w_uq, mla_kv_norm, w_ukv, w_branch, w_out, norm_ffn, ffn_w1, ffn_w3, ffn_w2, w_router, moe_w1, moe_w3, moe_w2, norm_final):
    p = dict(x=x, positions=positions, norm_mix=norm_mix, w_in=w_in, hg_lb_logits=hg_lb_logits, hg_norm=hg_norm,
             mla_q_norm=mla_q_norm, w_uq=w_uq, mla_kv_norm=mla_kv_norm, w_ukv=w_ukv, w_branch=w_branch, w_out=w_out,
             norm_ffn=norm_ffn, ffn_w1=ffn_w1, ffn_w3=ffn_w3, ffn_w2=ffn_w2, w_router=w_router, moe_w1=moe_w1,
             moe_w3=moe_w3, moe_w2=moe_w2, norm_final=norm_final)
    return forward(p, Dims())
```

```python
import functools
from typing import NamedTuple

import jax
import jax.numpy as jnp
from jax import lax
from jax.experimental import pallas as pl
from jax.experimental.pallas import tpu as pltpu

F32 = jnp.float32
BF16 = jnp.bfloat16

LANES = 128
V7X_VMEM_BYTES = 64 * 1024 * 1024
VMEM_LIMIT_BYTES = V7X_VMEM_BYTES - 8 * 1024 * 1024


class Dims(NamedTuple):
    d_model: int = 4096
    batch: int = 2
    seq: int = 8192
    depth: int = 2
    chunk: int = 64
    eps: float = 1e-6
    hg_dk: int = 128
    hg_heads: int = 16
    hg_dv: int = 128
    mla_dv: int = 128
    mla_heads: int = 16
    mla_nope: int = 128
    mla_rope: int = 64
    mla_q_rank: int = 768
    mla_kv_rank: int = 512
    rope_theta: float = 10000.0
    d_ff: int = 11008
    n_experts: int = 8
    d_ff_expert: int = 4096
    tm: int = 1024
    tn: int = 1024
    tn2: int = 512
    tk_ffn: int = 2816
    d_ff_pad: int = 11264
    tm_small: int = 512
    tm_norm: int = 256
    tm_expert: int = 512
    tm_combine: int = 256
    tm_up: int = 1024
    tq: int = 2048
    hgrn_rows: int = 1024


LOG2_E = 1.4426950408889634
HG_CHUNK = 64


def _cparams(semantics):
    return pltpu.CompilerParams(dimension_semantics=semantics, vmem_limit_bytes=VMEM_LIMIT_BYTES)


def _sigmoid(x):
    return 1.0 / (1.0 + jnp.exp(-x))


def _silu(x):
    return x * _sigmoid(x)


def _rmsnorm_kernel(x_ref, g_ref, o_ref, *, eps):
    x = x_ref[...]
    ms = jnp.mean(x * x, axis=-1, keepdims=True)
    o_ref[...] = (x * lax.rsqrt(ms + eps) * g_ref[...]).astype(o_ref.dtype)


def rmsnorm(x, g, out_dtype, *, tm, eps):
    n, d = x.shape
    return pl.pallas_call(
        functools.partial(_rmsnorm_kernel, eps=eps),
        grid=(n // tm,),
        in_specs=[pl.BlockSpec((tm, d), lambda i: (i, 0)), pl.BlockSpec((1, d), lambda i: (0, 0))],
        out_specs=pl.BlockSpec((tm, d), lambda i: (i, 0)),
        out_shape=jax.ShapeDtypeStruct((n, d), out_dtype),
        compiler_params=_cparams(("parallel",)),
        name="rmsnorm",
    )(x, g.reshape(1, d))


def _mm_kernel(*refs, n_a, w_src, n_extra, n_out, nk, epilogue):
    n_w = len(w_src)
    a_refs = refs[:n_a]
    w_refs = refs[n_a:n_a + n_w]
    e_refs = refs[n_a + n_w:n_a + n_w + n_extra]
    o_refs = refs[n_a + n_w + n_extra:n_a + n_w + n_extra + n_out]
    acc_refs = refs[n_a + n_w + n_extra + n_out:]
    parts = [jnp.dot(a_refs[s][...], w[...], preferred_element_type=F32) for w, s in zip(w_refs, w_src)]
    if nk == 1:
        epilogue(parts, e_refs, o_refs)
        return
    k = pl.program_id(2)

    @pl.when(k == 0)
    def _():
        for acc, p in zip(acc_refs, parts):
            acc[...] = p

    @pl.when(k > 0)
    def _():
        for acc, p in zip(acc_refs, parts):
            acc[...] += p

    @pl.when(k == nk - 1)
    def _():
        epilogue([acc[...] for acc in acc_refs], e_refs, o_refs)


def matmul(a_list, w_list, w_src, extras, extra_specs, out_shapes, out_specs, epilogue, *, tm, tn, tk=None, name):
    m, kdim = a_list[0].shape
    n = w_list[0].shape[1]
    tk = kdim if tk is None else tk
    nk = kdim // tk
    assert m % tm == 0 and n % tn == 0 and kdim % tk == 0
    in_specs = [pl.BlockSpec((tm, tk), lambda i, j, k: (i, k)) for _ in a_list]
    in_specs += [pl.BlockSpec((tk, tn), lambda i, j, k: (k, j)) for _ in w_list]
    in_specs += list(extra_specs)
    scratch = [pltpu.VMEM((tm, tn), F32) for _ in w_list] if nk > 1 else []
    return pl.pallas_call(
        functools.partial(_mm_kernel, n_a=len(a_list), w_src=tuple(w_src), n_extra=len(extras),
                          n_out=len(out_shapes), nk=nk, epilogue=epilogue),
        grid=(m // tm, n // tn, nk),
        in_specs=in_specs,
        out_specs=list(out_specs),
        out_shape=list(out_shapes),
        scratch_shapes=scratch,
        compiler_params=_cparams(("parallel", "parallel", "arbitrary")),
        name=name,
    )(*a_list, *w_list, *extras)


def _tile_spec(tm, tn, col_block_offset=0):
    return pl.BlockSpec((tm, tn), lambda i, j, k: (i, j + col_block_offset))


def _row_spec(tm, width):
    return pl.BlockSpec((tm, width), lambda i, j, k: (i, 0))


def _ep_store(parts, e_refs, o_refs):
    o_refs[0][...] = parts[0].astype(o_refs[0].dtype)


def _ep_residual(parts, e_refs, o_refs):
    o_refs[0][...] = e_refs[0][...] + parts[0]


def _ep_swiglu(parts, e_refs, o_refs):
    o_refs[0][...] = (_silu(parts[0]) * parts[1]).astype(o_refs[0].dtype)


def _ep_gated_sum(parts, e_refs, o_refs):
    ga = e_refs[0][...].astype(F32)
    gb = e_refs[1][...].astype(F32)
    o_refs[0][...] = (_sigmoid(ga) * parts[0] + _sigmoid(gb) * parts[1]).astype(o_refs[0].dtype)


def _rope128(g, c, s1, s2):
    return g * c + pltpu.roll(g, 32, 1) * s1 + pltpu.roll(g, LANES - 32, 1) * s2


def _ep_mla_pre(parts, e_refs, o_refs, *, q_rank, kv_rank, eps):
    acc = parts[0]
    qg, kvg, c, s1, s2 = (r[...] for r in e_refs)
    cq = acc[:, :q_rank]
    ckv = acc[:, q_rank:q_rank + kv_rank]
    kp = acc[:, q_rank + kv_rank:q_rank + kv_rank + LANES]
    cq = cq * lax.rsqrt(jnp.mean(cq * cq, axis=-1, keepdims=True) + eps) * qg
    ckv = ckv * lax.rsqrt(jnp.mean(ckv * ckv, axis=-1, keepdims=True) + eps) * kvg
    o_refs[0][...] = cq.astype(o_refs[0].dtype)
    o_refs[1][...] = ckv.astype(o_refs[1].dtype)
    o_refs[2][...] = _rope128(kp, c, s1, s2).astype(o_refs[2].dtype)


def _ep_mla_q(parts, e_refs, o_refs, *, heads, scale):
    acc = parts[0]
    c, s1, s2 = (r[...] for r in e_refs)
    for h in range(heads):
        base = 2 * LANES * h
        o_refs[0][h, :, :LANES] = (acc[:, base:base + LANES] * scale).astype(o_refs[0].dtype)
        pe = _rope128(acc[:, base + LANES:base + 2 * LANES], c, s1, s2)
        o_refs[0][h, :, LANES:] = (pe * scale).astype(o_refs[0].dtype)


def _ep_mla_kv(parts, e_refs, o_refs, *, heads):
    acc = parts[0]
    kpe = e_refs[0][...]
    for h in range(heads):
        base = 2 * LANES * h
        o_refs[0][h, :, :LANES] = acc[:, base:base + LANES].astype(o_refs[0].dtype)
        o_refs[0][h, :, LANES:] = kpe
        o_refs[1][h] = acc[:, base + LANES:base + 2 * LANES].astype(o_refs[1].dtype)


ATTN_SPLIT = 2


def _attn_kernel(q_ref, k_ref, v_ref, o_ref, *, tq, chunk):
    qi = pl.program_id(2)
    dv = v_ref.shape[-1]
    rows = tq // ATTN_SPLIT
    qs = [q_ref[pl.ds(i * rows, rows), :] for i in range(ATTN_SPLIT)]
    r_chunk = lax.broadcasted_iota(jnp.int32, (rows, rows), 0) // chunk
    c_chunk = lax.broadcasted_iota(jnp.int32, (rows, rows), 1) // chunk
    diag_mask = c_chunk <= r_chunk

    def update(q, k, v, carry, mask):
        m, l, acc = carry
        s = lax.dot_general(q, k, (((1,), (1,)), ((), ())), preferred_element_type=F32)
        if mask is not None:
            s = jnp.where(mask, s, -jnp.inf)
        m_new = jnp.maximum(m, jnp.max(s, axis=1, keepdims=True))
        p = jnp.exp2(s - m_new[:, :1])
        alpha = jnp.exp2(m - m_new)
        l = alpha * l + jnp.sum(p, axis=1, keepdims=True)
        acc = alpha[:, :dv] * acc + jnp.dot(p.astype(BF16), v, preferred_element_type=F32)
        return m_new, l, acc

    def full_block(kb, carry):
        ks = pl.multiple_of(kb * tq, tq)
        k = k_ref[pl.ds(ks, tq), :]
        v = v_ref[pl.ds(ks, tq), :]
        return tuple(update(qs[i], k, v, carry[i], None) for i in range(ATTN_SPLIT))

    init = tuple((jnp.full((rows, LANES), -jnp.inf, F32), jnp.zeros((rows, LANES), F32), jnp.zeros((rows, dv), F32))
                 for _ in range(ATTN_SPLIT))
    carry = list(lax.fori_loop(0, qi, full_block, init))
    for i in range(ATTN_SPLIT):
        for d in range(i + 1):
            ks = pl.multiple_of(qi * tq + d * rows, rows)
            k = k_ref[pl.ds(ks, rows), :]
            v = v_ref[pl.ds(ks, rows), :]
            carry[i] = update(qs[i], k, v, carry[i], diag_mask if d == i else None)
    for i in range(ATTN_SPLIT):
        _, l, acc = carry[i]
        o_ref[pl.ds(i * rows, rows), :] = (acc / l[:, :dv]).astype(o_ref.dtype)


def attention(q_full, k_full, v, *, tq, chunk):
    b, h, s, dq = q_full.shape
    dv = v.shape[-1]
    assert s % tq == 0 and (tq // ATTN_SPLIT) % chunk == 0 and dv == LANES
    return pl.pallas_call(
        functools.partial(_attn_kernel, tq=tq, chunk=chunk),
        grid=(b, h, s // tq),
        in_specs=[
            pl.BlockSpec((None, None, tq, dq), lambda bi, hi, qi: (bi, hi, qi, 0)),
            pl.BlockSpec((None, None, s, dq), lambda bi, hi, qi: (bi, hi, 0, 0)),
            pl.BlockSpec((None, None, s, dv), lambda bi, hi, qi: (bi, hi, 0, 0)),
        ],
        out_specs=pl.BlockSpec((None, tq, dv), lambda bi, hi, qi: (bi, qi, hi)),
        out_shape=jax.ShapeDtypeStruct((b, s, h * dv), BF16),
        compiler_params=_cparams(("parallel", "parallel", "arbitrary")),
        name="mla_attention",
    )(q_full, k_full, v)


def _split3(x):
    hi = x.astype(BF16)
    r = x - hi.astype(F32)
    mid = r.astype(BF16)
    lo = (r - mid.astype(F32)).astype(BF16)
    return hi, mid, lo


def _hgrn_kernel(q_ref, f_ref, i_ref, og_ref, lb_ref, gain_ref, o_ref, state_ref, *, rows, eps):
    @pl.when(pl.program_id(2) == 0)
    def _():
        state_ref[...] = jnp.zeros_like(state_ref)

    dk = q_ref.shape[-1]
    dv = i_ref.shape[-1]
    lb = lb_ref[...]
    log_lb = jnp.log(lb)
    log_1m_lb = jnp.log1p(-lb)
    one_m_lb = 1.0 - lb
    gain = gain_ref[...]
    cl = HG_CHUNK
    halves = [1 << l for l in range(cl.bit_length() - 1)]
    chunks = [slice(c, c + cl) for c in range(0, rows, cl)]
    ones_k = jnp.ones((dk, LANES), BF16)
    ones_v = jnp.full((dv, LANES), 1.0 / dv, F32).astype(BF16)

    t_id = lax.broadcasted_iota(jnp.int32, (cl, cl), 0)
    s_id = lax.broadcasted_iota(jnp.int32, (cl, cl), 1)
    row = lax.broadcasted_iota(jnp.int32, (cl, dk), 0)
    cum_rows = [s_id <= t_id] + [s_id <= (t_id // (2 * h)) * (2 * h) + h for h in halves]
    cum_rows = jnp.concatenate([jnp.where(m, 1.0, 0.0) for m in cum_rows], axis=0).astype(BF16)
    upper = [(row % (2 * h)) >= h for h in halves]
    pair = [(t_id // (2 * h) == s_id // (2 * h)) & (t_id % (2 * h) >= h) & (s_id % (2 * h) < h) for h in halves]

    z = f_ref[...]
    y = log_1m_lb + (jnp.minimum(z, 0.0) - jnp.log1p(jnp.exp(-jnp.abs(z))))
    g = jnp.maximum(log_lb, y) + jnp.log1p(jnp.exp(-jnp.abs(log_lb - y)))
    kk = one_m_lb / (1.0 + jnp.exp(z))
    q = _silu(q_ref[...])
    v = i_ref[...]
    v16 = v.astype(BF16)
    g3 = jnp.concatenate(_split3(g), axis=1)

    cums = []
    for sl in chunks:
        c3 = jnp.dot(cum_rows, g3[sl], preferred_element_type=F32)
        cums.append(c3[:, :dk] + c3[:, dk:2 * dk] + c3[:, 2 * dk:])

    scores = []
    for sl, cum in zip(chunks, cums):
        b = cum[:cl]
        qc, kc = q[sl], kk[sl]
        acc = None
        for l, h in enumerate(halves):
            e = jnp.exp(-jnp.abs(b - cum[cl * (l + 1):cl * (l + 2)]))
            x = (jnp.where(upper[l], qc, kc) * e).astype(BF16)
            p = lax.dot_general(x, x, (((1,), (1,)), ((), ())), preferred_element_type=F32)
            p = jnp.where(pair[l], p, 0.0)
            acc = p if acc is None else acc + p
        scores.append(acc.astype(BF16))

    incs, decs, q_decs = [], [], []
    for sl, cum in zip(chunks, cums):
        b = cum[:cl]
        b_last = b[cl - 1:cl, :]
        k_dec = kk[sl] * jnp.exp(b_last - b)
        incs.append(lax.dot_general(v16[sl], k_dec.astype(BF16), (((0,), (0,)), ((), ())),
                                    preferred_element_type=F32))
        decs.append(jnp.exp(b_last))
        q_decs.append((q[sl] * jnp.exp(b)).astype(BF16))
    st = state_ref[...]
    o_inter = []
    for j in range(len(chunks)):
        o_inter.append(lax.dot_general(q_decs[j], st.astype(BF16), (((1,), (1,)), ((), ())),
                                       preferred_element_type=F32))
        st = st * decs[j] + incs[j]
    state_ref[...] = st

    outs = []
    for j, sl in enumerate(chunks):
        diag = jnp.dot((q[sl] * kk[sl]).astype(BF16), ones_k, preferred_element_type=F32)[:, :dv]
        outs.append(o_inter[j] + jnp.dot(scores[j], v16[sl], preferred_element_type=F32) + diag * v[sl])
    o = jnp.concatenate(outs, axis=0)
    ms = jnp.dot((o * o).astype(BF16), ones_v, preferred_element_type=F32)[:, :dv]
    o = o * lax.rsqrt(ms + eps) * gain * _silu(og_ref[...])
    o_ref[...] = o.astype(o_ref.dtype)


def hgrn2(hg, lb, gain, *, heads, dk, dv, rows, eps):
    b, s, _ = hg.shape
    assert dk == LANES and dv == LANES and s % rows == 0 and rows % HG_CHUNK == 0

    def col(group):
        return pl.BlockSpec((None, rows, dk), lambda bi, hi, si: (bi, si, group * heads + hi))

    vec = pl.BlockSpec((1, dk), lambda bi, hi, si: (0, hi))
    return pl.pallas_call(
        functools.partial(_hgrn_kernel, rows=rows, eps=eps),
        grid=(b, heads, s // rows),
        in_specs=[col(0), col(1), col(2), col(3), vec, vec],
        out_specs=pl.BlockSpec((None, rows, dv), lambda bi, hi, si: (bi, si, hi)),
        out_shape=jax.ShapeDtypeStruct((b, s, heads * dv), BF16),
        scratch_shapes=[pltpu.VMEM((dv, dk), F32)],
        compiler_params=_cparams(("parallel", "parallel", "arbitrary")),
        name="hgrn2",
    )(hg, hg, hg, hg, lb.reshape(1, -1), gain.reshape(1, -1))


META_E1, META_E2, META_R1, META_R2, META_W1, META_W2 = range(6)
HI16 = 0xFFFF0000


def _pack_bf16_pair(lo, hi):
    lo_bits = lax.bitcast_convert_type(lo.astype(jnp.bfloat16).astype(F32), jnp.uint32) >> 16
    hi_bits = lax.bitcast_convert_type(hi.astype(jnp.bfloat16).astype(F32), jnp.uint32)
    return hi_bits | lo_bits


def _unpack_bf16_pair(word):
    lo = lax.bitcast_convert_type(word << 16, F32).astype(BF16)
    hi = lax.bitcast_convert_type(word & jnp.uint32(HI16), F32).astype(BF16)
    return lo, hi


def _router_kernel(x_ref, g_ref, wr_ref, hp_ref, meta_ref, cnt_ref, base_ref, *, n_experts, eps):
    @pl.when(pl.program_id(0) == 0)
    def _():
        base_ref[...] = jnp.zeros_like(base_ref)

    tm, d = x_ref.shape
    half = d // 2
    x = x_ref[...]
    h = x * lax.rsqrt(jnp.mean(x * x, axis=-1, keepdims=True) + eps) * g_ref[...]
    packed = _pack_bf16_pair(h[:, :half], h[:, half:])
    slab = half // LANES
    for a in range(slab):
        hp_ref[pl.ds(a, tm, stride=slab), :] = packed[:, a * LANES:(a + 1) * LANES]

    logits = jnp.dot(h, wr_ref[...], preferred_element_type=F32, precision=lax.Precision.HIGHEST)
    lane = lax.broadcasted_iota(jnp.int32, logits.shape, 1)
    logits = jnp.where(lane < n_experts, logits, -jnp.inf)
    m1 = jnp.max(logits, axis=1, keepdims=True)
    i1 = jnp.min(jnp.where(logits == m1, lane, LANES), axis=1, keepdims=True)
    rest = jnp.where(lane == i1, -jnp.inf, logits)
    m2 = jnp.max(rest, axis=1, keepdims=True)
    i2 = jnp.min(jnp.where(rest == m2, lane, LANES), axis=1, keepdims=True)
    e2 = jnp.exp(m2 - m1)
    w1 = 1.0 / (1.0 + e2)
    w2 = e2 / (1.0 + e2)

    sel1 = lane == i1
    sel2 = lane == i2
    sel = jnp.where(sel1 | sel2, 1.0, 0.0)
    r_id = lax.broadcasted_iota(jnp.int32, (tm, tm), 0)
    c_id = lax.broadcasted_iota(jnp.int32, (tm, tm), 1)
    before = jnp.where(c_id < r_id, 1.0, 0.0).astype(BF16)
    rank = base_ref[...] + jnp.dot(before, sel.astype(BF16), preferred_element_type=F32)
    r1 = jnp.sum(jnp.where(sel1, rank, 0.0), axis=1, keepdims=True)
    r2 = jnp.sum(jnp.where(sel2, rank, 0.0), axis=1, keepdims=True)
    base_ref[...] += jnp.sum(sel, axis=0, keepdims=True)
    cnt_ref[...] = jnp.broadcast_to(base_ref[...], cnt_ref.shape)

    record = jnp.zeros(logits.shape, F32)
    for lane_id, val in ((META_E1, i1.astype(F32)), (META_E2, i2.astype(F32)), (META_R1, r1), (META_R2, r2),
                         (META_W1, w1), (META_W2, w2)):
        record = jnp.where(lane == lane_id, val, record)
    meta_ref[...] = record


def router(x, g, w_router, *, tm, eps):
    n, d = x.shape
    n_experts = w_router.shape[1]
    slab = d // 2 // LANES
    wr = jnp.zeros((d, LANES), F32).at[:, :n_experts].set(w_router)
    return pl.pallas_call(
        functools.partial(_router_kernel, n_experts=n_experts, eps=eps),
        grid=(n // tm,),
        in_specs=[pl.BlockSpec((tm, d), lambda i: (i, 0)), pl.BlockSpec((1, d), lambda i: (0, 0)),
                  pl.BlockSpec((d, LANES), lambda i: (0, 0))],
        out_specs=[pl.BlockSpec((tm * slab, LANES), lambda i: (i, 0)), pl.BlockSpec((tm, LANES), lambda i: (i, 0)),
                   pl.BlockSpec((8, LANES), lambda i: (0, 0))],
        out_shape=[jax.ShapeDtypeStruct((n * slab, LANES), jnp.uint32), jax.ShapeDtypeStruct((n, LANES), F32),
                   jax.ShapeDtypeStruct((8, LANES), F32)],
        scratch_shapes=[pltpu.VMEM((1, LANES), F32)],
        compiler_params=_cparams(("arbitrary",)),
        name="ffn_norm_router",
    )(x, g.reshape(1, d), wr)


def _row_copy(src_ref, src_row, dst_ref, dst_row, slab, sem):
    return pltpu.make_async_copy(src_ref.at[pl.ds(src_row * slab, slab), :],
                                 dst_ref.at[pl.ds(dst_row * slab, slab), :], sem)


def _dispatch_kernel(dest_ref, h_ref, zeros_ref, xs_ref, sem, *, tokens, slab):
    del zeros_ref
    base = pl.program_id(0) * tokens

    def issue(t, carry):
        for s in range(2):
            _row_copy(h_ref, t, xs_ref, dest_ref[2 * (base + t) + s], slab, sem).start()
        return carry

    def drain(t, carry):
        for s in range(2):
            _row_copy(h_ref, 0, xs_ref, 0, slab, sem).wait()
        return carry

    lax.fori_loop(0, tokens, issue, 0)
    lax.fori_loop(0, tokens, drain, 0)


def dispatch(dest, hp, rows_padded, *, tokens, slab):
    n = hp.shape[0] // slab
    zeros = jnp.zeros((rows_padded * slab, LANES), hp.dtype)
    return pl.pallas_call(
        functools.partial(_dispatch_kernel, tokens=tokens, slab=slab),
        grid_spec=pltpu.PrefetchScalarGridSpec(
            num_scalar_prefetch=1, grid=(n // tokens,),
            in_specs=[pl.BlockSpec((tokens * slab, LANES), lambda i, dest: (i, 0)),
                      pl.BlockSpec(memory_space=pl.ANY)],
            out_specs=pl.BlockSpec(memory_space=pl.ANY),
            scratch_shapes=[pltpu.SemaphoreType.DMA(())]),
        out_shape=jax.ShapeDtypeStruct(zeros.shape, zeros.dtype),
        input_output_aliases={2: 0},
        compiler_params=_cparams(("arbitrary",)),
        name="moe_dispatch",
    )(dest, hp, zeros)


def _expert_up_kernel(te_ref, nu_ref, x_ref, w1_ref, w3_ref, u_ref, xb_ref, *, slab):
    i, j = pl.program_id(0), pl.program_id(1)
    used = i < nu_ref[0]
    tm = xb_ref.shape[0]
    half = slab * LANES

    @pl.when(used & (j == 0))
    def _():
        for a in range(slab):
            lo, hi = _unpack_bf16_pair(x_ref[pl.ds(a, tm, stride=slab), :])
            xb_ref[:, a * LANES:(a + 1) * LANES] = lo
            xb_ref[:, half + a * LANES:half + (a + 1) * LANES] = hi

    @pl.when(used)
    def _():
        xb = xb_ref[...]
        a1 = jnp.dot(xb, w1_ref[...], preferred_element_type=F32)
        a3 = jnp.dot(xb, w3_ref[...], preferred_element_type=F32)
        u_ref[...] = (_silu(a1) * a3).astype(u_ref.dtype)

    @pl.when(jnp.logical_not(used))
    def _():
        u_ref[...] = jnp.zeros_like(u_ref)


def _expert_down_kernel(te_ref, nu_ref, u_ref, w2_ref, y_ref, *, slab):
    i, j = pl.program_id(0), pl.program_id(1)
    used = i < nu_ref[0]
    tm, tn = u_ref.shape[0], w2_ref.shape[1]
    half = tn // 2
    chunks = half // LANES

    @pl.when(used)
    def _():
        acc = jnp.dot(u_ref[...], w2_ref[...], preferred_element_type=F32)
        packed = _pack_bf16_pair(acc[:, :half], acc[:, half:])
        for c in range(chunks):
            y_ref[pl.ds(j * chunks + c, tm, stride=slab), :] = packed[:, c * LANES:(c + 1) * LANES]

    @pl.when(jnp.logical_not(used) & (j == 0))
    def _():
        y_ref[...] = jnp.zeros_like(y_ref)


def expert_ffn(tile_expert, n_used, xs, w1, w3, w2, *, tm, tn):
    n_exp, d, dff = w1.shape
    in_slab = d // 2 // LANES
    out_slab = d // 2 // LANES
    rows = xs.shape[0] // in_slab
    n_tiles = rows // tm

    def tile(i, nu):
        return jnp.minimum(i, nu[0] - 1)

    u = pl.pallas_call(
        functools.partial(_expert_up_kernel, slab=in_slab),
        grid_spec=pltpu.PrefetchScalarGridSpec(
            num_scalar_prefetch=2, grid=(n_tiles, dff // tn),
            in_specs=[pl.BlockSpec((tm * in_slab, LANES), lambda i, j, te, nu: (tile(i, nu), 0)),
                      pl.BlockSpec((None, d, tn), lambda i, j, te, nu: (te[i], 0, j)),
                      pl.BlockSpec((None, d, tn), lambda i, j, te, nu: (te[i], 0, j))],
            out_specs=pl.BlockSpec((tm, tn), lambda i, j, te, nu: (i, j)),
            scratch_shapes=[pltpu.VMEM((tm, d), BF16)]),
        out_shape=jax.ShapeDtypeStruct((rows, dff), BF16),
        compiler_params=_cparams(("arbitrary", "arbitrary")),
        name="moe_expert_up",
    )(tile_expert, n_used, xs, w1, w3)
    return pl.pallas_call(
        functools.partial(_expert_down_kernel, slab=out_slab),
        grid_spec=pltpu.PrefetchScalarGridSpec(
            num_scalar_prefetch=2, grid=(n_tiles, d // tn),
            in_specs=[pl.BlockSpec((tm, dff), lambda i, j, te, nu: (tile(i, nu), 0)),
                      pl.BlockSpec((None, dff, tn), lambda i, j, te, nu: (te[i], 0, j))],
            out_specs=pl.BlockSpec((tm * out_slab, LANES), lambda i, j, te, nu: (i, 0))),
        out_shape=jax.ShapeDtypeStruct((rows * out_slab, LANES), jnp.uint32),
        compiler_params=_cparams(("arbitrary", "arbitrary")),
        name="moe_expert_down",
    )(tile_expert, n_used, u, w2)


def _combine_kernel(dest_ref, x_ref, meta_ref, g_ref, y_ref, o_ref, buf_ref, sem, *, tokens, slab, pair_block, eps,
                    final_norm):
    i = pl.program_id(0)
    slot = i % 2

    def fetch(step, into):
        def issue(t, carry):
            for s in range(2):
                pltpu.make_async_copy(y_ref.at[pl.ds(dest_ref[2 * (step * tokens + t) + s] * slab, slab), :],
                                      buf_ref.at[into, s, pl.ds(t * slab, slab), :], sem.at[into]).start()
            return carry

        lax.fori_loop(0, tokens, issue, 0)

    @pl.when(i == 0)
    def _():
        fetch(0, 0)

    @pl.when(i + 1 < pl.num_programs(0))
    def _():
        fetch(i + 1, 1 - slot)

    def drain(t, carry):
        for s in range(2):
            pltpu.make_async_copy(y_ref.at[pl.ds(0, slab), :], buf_ref.at[slot, s, pl.ds(0, slab), :],
                                  sem.at[slot]).wait()
        return carry

    lax.fori_loop(0, tokens, drain, 0)

    meta = meta_ref[...]
    w1 = meta[:, META_W1:META_W1 + 1]
    w2 = meta[:, META_W2:META_W2 + 1]
    w1 = jnp.broadcast_to(w1, (tokens, LANES))
    w2 = jnp.broadcast_to(w2, (tokens, LANES))
    sq = jnp.zeros((tokens, LANES), F32)
    half = pair_block // 2
    chunks = half // LANES
    for c in range(slab):
        lo_col = (c // chunks) * pair_block + (c % chunks) * LANES
        y1 = buf_ref[slot, 0, pl.ds(c, tokens, stride=slab), :]
        y2 = buf_ref[slot, 1, pl.ds(c, tokens, stride=slab), :]
        for col, shift in ((lo_col, True), (lo_col + half, False)):
            cols = slice(col, col + LANES)
            a1 = lax.bitcast_convert_type(y1 << 16 if shift else y1 & jnp.uint32(HI16), F32)
            a2 = lax.bitcast_convert_type(y2 << 16 if shift else y2 & jnp.uint32(HI16), F32)
            o = x_ref[:, cols] + w1 * a1 + w2 * a2
            o_ref[:, cols] = o
            sq = sq + o * o
    if final_norm:
        d = 2 * slab * LANES
        inv = jnp.broadcast_to(lax.rsqrt(jnp.sum(sq, axis=1, keepdims=True) / d + eps), (tokens, LANES))
        for c in range(2 * slab):
            cols = slice(c * LANES, (c + 1) * LANES)
            o_ref[:, cols] = o_ref[:, cols] * inv * g_ref[:, cols]


def combine(dest, x, meta, gain, y, *, tokens, pair_block, eps, final_norm):
    n, d = x.shape
    slab = d // 2 // LANES
    return pl.pallas_call(
        functools.partial(_combine_kernel, tokens=tokens, slab=slab, pair_block=pair_block, eps=eps,
                          final_norm=final_norm),
        grid_spec=pltpu.PrefetchScalarGridSpec(
            num_scalar_prefetch=1, grid=(n // tokens,),
            in_specs=[pl.BlockSpec((tokens, d), lambda i, dest: (i, 0)),
                      pl.BlockSpec((tokens, LANES), lambda i, dest: (i, 0)),
                      pl.BlockSpec((1, d), lambda i, dest: (0, 0)),
                      pl.BlockSpec(memory_space=pl.ANY)],
            out_specs=pl.BlockSpec((tokens, d), lambda i, dest: (i, 0)),
            scratch_shapes=[pltpu.VMEM((2, 2, tokens * slab, LANES), jnp.uint32), pltpu.SemaphoreType.DMA((2,))]),
        out_shape=jax.ShapeDtypeStruct((n, d), F32),
        compiler_params=_cparams(("arbitrary",)),
        name="moe_combine",
    )(dest, x, meta, gain.reshape(1, d), y)


def _rope_tables(positions, dm):
    half = dm.mla_rope // 2
    inv_freq = dm.rope_theta ** (-jnp.arange(half, dtype=F32) / half)
    ang = positions.astype(F32).reshape(-1, 1) * inv_freq
    cos, sin = jnp.cos(ang), jnp.sin(ang)
    zero = jnp.zeros_like(cos)
    c = jnp.concatenate([cos, cos, zero, zero], axis=-1)
    s1 = jnp.concatenate([zero, sin, zero, zero], axis=-1)
    s2 = jnp.concatenate([-sin, zero, zero, zero], axis=-1)
    return c, s1, s2


def _mixer(x, lp, tables, dm):
    n = dm.batch * dm.seq
    d = dm.d_model
    hg_cols = 4 * dm.hg_heads * dm.hg_dk
    c, s1, s2 = tables
    h = rmsnorm(x, lp["norm_mix"], BF16, tm=dm.tm_norm, eps=dm.eps)

    (hg,) = matmul([h], [lp["w_hg"]], [0], [], [], [jax.ShapeDtypeStruct((n, hg_cols), F32)],
                   [_tile_spec(dm.tm, dm.tn)], _ep_store, tm=dm.tm, tn=dm.tn, name="proj_hgrn")
    (gates,) = matmul([h], [lp["w_gates"]], [0], [], [], [jax.ShapeDtypeStruct((n, 2 * d), BF16)],
                      [_tile_spec(dm.tm, dm.tn)], _ep_store, tm=dm.tm, tn=dm.tn, name="proj_gates")

    mla_cols = lp["w_mla"].shape[1]
    tms = dm.tm_small
    cq, ckv, kpe = matmul(
        [h], [lp["w_mla"]], [0],
        [lp["q_gain"], lp["kv_gain"], c, s1, s2],
        [pl.BlockSpec((1, dm.mla_q_rank), lambda i, j, k: (0, 0)),
         pl.BlockSpec((1, dm.mla_kv_rank), lambda i, j, k: (0, 0)),
         _row_spec(tms, LANES), _row_spec(tms, LANES), _row_spec(tms, LANES)],
        [jax.ShapeDtypeStruct((n, dm.mla_q_rank), BF16), jax.ShapeDtypeStruct((n, dm.mla_kv_rank), BF16),
         jax.ShapeDtypeStruct((n, LANES), BF16)],
        [_row_spec(tms, dm.mla_q_rank), _row_spec(tms, dm.mla_kv_rank), _row_spec(tms, LANES)],
        functools.partial(_ep_mla_pre, q_rank=dm.mla_q_rank, kv_rank=dm.mla_kv_rank, eps=dm.eps),
        tm=tms, tn=mla_cols, name="proj_mla_latents")

    heads = dm.mla_heads
    hb = min(4, heads)
    tmu = dm.tm_up
    s_tiles = dm.seq // tmu

    def head_spec(width):
        return pl.BlockSpec((None, hb, tmu, width), lambda i, j, k: (i // s_tiles, j, i % s_tiles, 0))

    scale = (dm.mla_nope + dm.mla_rope) ** -0.5 * LOG2_E
    (q_full,) = matmul(
        [cq], [lp["w_uq"]], [0], [c, s1, s2],
        [_row_spec(tmu, LANES), _row_spec(tmu, LANES), _row_spec(tmu, LANES)],
        [jax.ShapeDtypeStruct((dm.batch, heads, dm.seq, 2 * LANES), BF16)], [head_spec(2 * LANES)],
        functools.partial(_ep_mla_q, heads=hb, scale=scale), tm=tmu, tn=hb * 2 * LANES, name="mla_q_up")
    k_full, v = matmul(
        [ckv], [lp["w_ukv"]], [0], [kpe], [_row_spec(tmu, LANES)],
        [jax.ShapeDtypeStruct((dm.batch, heads, dm.seq, 2 * LANES), BF16),
         jax.ShapeDtypeStruct((dm.batch, heads, dm.seq, dm.mla_dv), BF16)],
        [head_spec(2 * LANES), head_spec(dm.mla_dv)],
        functools.partial(_ep_mla_kv, heads=hb), tm=tmu, tn=hb * 2 * LANES, name="mla_kv_up")
    o_b = attention(q_full, k_full, v, tq=dm.tq, chunk=dm.chunk).reshape(n, heads * dm.mla_dv)

    o_a = hgrn2(hg.reshape(dm.batch, dm.seq, hg_cols), lp["lb"], lp["hg_gain"], heads=dm.hg_heads, dk=dm.hg_dk,
                dv=dm.hg_dv, rows=dm.hgrn_rows, eps=dm.eps).reshape(n, dm.hg_heads * dm.hg_dv)

    (y,) = matmul([o_a, o_b], [lp["w_branch_a"], lp["w_branch_b"]], [0, 1], [gates, gates],
                  [_tile_spec(dm.tm, dm.tn2), _tile_spec(dm.tm, dm.tn2, d // dm.tn2)],
                  [jax.ShapeDtypeStruct((n, d), BF16)], [_tile_spec(dm.tm, dm.tn2)], _ep_gated_sum,
                  tm=dm.tm, tn=dm.tn2, name="branch_merge")
    (x,) = matmul([y], [lp["w_out"]], [0], [x], [_tile_spec(dm.tm, dm.tn2)],
                  [jax.ShapeDtypeStruct((n, d), F32)], [_tile_spec(dm.tm, dm.tn2)], _ep_residual,
                  tm=dm.tm, tn=dm.tn2, name="mixer_out")
    return x


def _dense_ffn(x, lp, dm):
    n, d = x.shape
    h = rmsnorm(x, lp["norm_ffn"], BF16, tm=dm.tm_norm, eps=dm.eps)
    (u,) = matmul([h], [lp["w1"], lp["w3"]], [0, 0], [], [], [jax.ShapeDtypeStruct((n, dm.d_ff_pad), BF16)],
                  [_tile_spec(dm.tm, dm.tn2)], _ep_swiglu, tm=dm.tm, tn=dm.tn2, name="ffn_up")
    (x,) = matmul([u], [lp["w2"]], [0], [x], [_tile_spec(dm.tm, dm.tn)], [jax.ShapeDtypeStruct((n, d), F32)],
                  [_tile_spec(dm.tm, dm.tn)], _ep_residual, tm=dm.tm, tn=dm.tn, tk=dm.tk_ffn, name="ffn_down")
    return x


def _moe_ffn(x, lp, dm, final_gain):
    n, d = x.shape
    n_exp, tme = dm.n_experts, dm.tm_expert
    hp, meta, cnt = router(x, lp["norm_ffn"], lp["w_router"], tm=dm.tm_norm, eps=dm.eps)
    counts = cnt[0, :n_exp].astype(jnp.int32)
    padded = (counts + tme - 1) // tme * tme
    ends = jnp.cumsum(padded)
    starts = ends - padded
    e12 = meta[:, META_E1:META_E2 + 1].astype(jnp.int32)
    r12 = meta[:, META_R1:META_R2 + 1].astype(jnp.int32)
    dest = (starts[e12] + r12).reshape(-1)
    n_tiles = (2 * n) // tme + n_exp
    n_used = (ends[-1] // tme).reshape(1)
    tile_row = jnp.minimum(jnp.arange(n_tiles, dtype=jnp.int32), n_used - 1) * tme
    tile_expert = jnp.minimum(jnp.sum(tile_row[:, None] >= ends[None, :], axis=1), n_exp - 1).astype(jnp.int32)

    in_slab = d // 2 // LANES
    xs = dispatch(dest, hp, n_tiles * tme, tokens=dm.tm_small, slab=in_slab)
    y = expert_ffn(tile_expert, n_used, xs, lp["w1"], lp["w3"], lp["w2"], tm=tme, tn=dm.tn2)
    gain = jnp.ones((d,), F32) if final_gain is None else final_gain
    return combine(dest, x, meta, gain, y, tokens=dm.tm_combine, pair_block=dm.tn2, eps=dm.eps,
                   final_norm=final_gain is not None)


def _prepare_layer(l, p, lbs, dm):
    d = dm.d_model
    hg_cols = 4 * dm.hg_heads * dm.hg_dk
    mla_in = dm.mla_q_rank + dm.mla_kv_rank + dm.mla_rope
    w_in = p["w_in"][l]
    w_mla = jnp.pad(w_in[:, hg_cols:hg_cols + mla_in], ((0, 0), (0, LANES - dm.mla_rope)))
    heads = dm.mla_heads
    w_uq = p["w_uq"][l].reshape(dm.mla_q_rank, heads, dm.mla_nope + dm.mla_rope)
    w_uq = jnp.pad(w_uq, ((0, 0), (0, 0), (0, 2 * LANES - dm.mla_nope - dm.mla_rope)))
    hg_width = dm.hg_heads * dm.hg_dv
    lp = {
        "norm_mix": p["norm_mix"][l],
        "w_hg": w_in[:, :hg_cols].astype(BF16),
        "w_mla": w_mla.astype(BF16),
        "w_gates": w_in[:, hg_cols + mla_in:].astype(BF16),
        "lb": lbs[l],
        "hg_gain": p["hg_norm"][l],
        "q_gain": p["mla_q_norm"][l].reshape(1, -1),
        "kv_gain": p["mla_kv_norm"][l].reshape(1, -1),
        "w_uq": w_uq.reshape(dm.mla_q_rank, heads * 2 * LANES).astype(BF16),
        "w_ukv": p["w_ukv"][l].astype(BF16),
        "w_branch_a": p["w_branch"][l, :hg_width].astype(BF16),
        "w_branch_b": p["w_branch"][l, hg_width:].astype(BF16),
        "w_out": p["w_out"][l].astype(BF16),
        "norm_ffn": p["norm_ffn"][l],
    }
    if l % 2 == 0:
        pad = dm.d_ff_pad - dm.d_ff
        lp["w1"] = jnp.pad(p["ffn_w1"][l // 2], ((0, 0), (0, pad))).astype(BF16)
        lp["w3"] = jnp.pad(p["ffn_w3"][l // 2], ((0, 0), (0, pad))).astype(BF16)
        lp["w2"] = jnp.pad(p["ffn_w2"][l // 2], ((0, pad), (0, 0))).astype(BF16)
    else:
        lp["w_router"] = p["w_router"][l // 2]
        lp["w1"] = p["moe_w1"][l // 2].astype(BF16)
        lp["w3"] = p["moe_w3"][l // 2].astype(BF16)
        lp["w2"] = p["moe_w2"][l // 2].astype(BF16)
    return lp


def forward(p, dm):
    n = dm.batch * dm.seq
    x = p["x"].reshape(n, dm.d_model)
    tables = _rope_tables(p["positions"], dm)
    lbs = jnp.cumsum(jax.nn.softmax(p["hg_lb_logits"].astype(F32), axis=0), axis=0)
    lbs = lbs - lbs[0:1]
    for l in range(dm.depth):
        lp = _prepare_layer(l, p, lbs, dm)
        x = _mixer(x, lp, tables, dm)
        last = l == dm.depth - 1
        if l % 2 == 0:
            x = _dense_ffn(x, lp, dm)
            if last:
                x = rmsnorm(x, p["norm_final"], F32, tm=dm.tm_norm, eps=dm.eps)
        else:
            x = _moe_ffn(x, lp, dm, p["norm_final"] if last else None)
    return x.reshape(dm.batch, dm.seq, dm.d_model)


def kernel(x, positions, norm_mix, w_in, hg_lb_logits, hg_norm, mla_q_norm, w_uq, mla_kv_norm, w_ukv, w_branch, w_out, norm_ffn, ffn_w1, ffn_w3, ffn_w2, w_router, moe_w1, moe_w3, moe_w2, norm_final):
    p = dict(x=x, positions=positions, norm_mix=norm_mix, w_in=w_in, hg_lb_logits=hg_lb_logits, hg_norm=hg_norm,
             mla_q_norm=mla_q_norm, w_uq=w_uq, mla_kv_norm=mla_kv_norm, w_ukv=w_ukv, w_branch=w_branch, w_out=w_out,
             norm_ffn=norm_ffn, ffn_w1=ffn_w1, ffn_w3=ffn_w3, ffn_w2=ffn_w2, w_router=w_router, moe_w1=moe_w1,
             moe_w3=moe_w3, moe_w2=moe_w2, norm_final=norm_final)
    return forward(p, Dims())
```

```python
import functools
from typing import NamedTuple

import jax
import jax.numpy as jnp
from jax import lax
from jax.experimental import pallas as pl
from jax.experimental.pallas import tpu as pltpu

F32 = jnp.float32
BF16 = jnp.bfloat16

LANES = 128
V7X_VMEM_BYTES = 64 * 1024 * 1024
VMEM_LIMIT_BYTES = V7X_VMEM_BYTES - 8 * 1024 * 1024


class Dims(NamedTuple):
    d_model: int = 4096
    batch: int = 2
    seq: int = 8192
    depth: int = 2
    chunk: int = 64
    eps: float = 1e-6
    hg_dk: int = 128
    hg_heads: int = 16
    hg_dv: int = 128
    mla_dv: int = 128
    mla_heads: int = 16
    mla_nope: int = 128
    mla_rope: int = 64
    mla_q_rank: int = 768
    mla_kv_rank: int = 512
    rope_theta: float = 10000.0
    d_ff: int = 11008
    n_experts: int = 8
    d_ff_expert: int = 4096
    tm: int = 1024
    tn: int = 1024
    tn2: int = 512
    tk_ffn: int = 2816
    d_ff_pad: int = 11264
    tm_small: int = 512
    tm_norm: int = 256
    tm_expert: int = 512
    tm_combine: int = 256
    tm_up: int = 1024
    tq: int = 2048
    hgrn_rows: int = 1024


LOG2_E = 1.4426950408889634
HG_CHUNK = 64


def _cparams(semantics):
    return pltpu.CompilerParams(dimension_semantics=semantics, vmem_limit_bytes=VMEM_LIMIT_BYTES)


def _sigmoid(x):
    return 1.0 / (1.0 + jnp.exp(-x))


def _silu(x):
    return x * _sigmoid(x)


def _rmsnorm_kernel(x_ref, g_ref, o_ref, *, eps):
    x = x_ref[...]
    ms = jnp.mean(x * x, axis=-1, keepdims=True)
    o_ref[...] = (x * lax.rsqrt(ms + eps) * g_ref[...]).astype(o_ref.dtype)


def rmsnorm(x, g, out_dtype, *, tm, eps):
    n, d = x.shape
    return pl.pallas_call(
        functools.partial(_rmsnorm_kernel, eps=eps),
        grid=(n // tm,),
        in_specs=[pl.BlockSpec((tm, d), lambda i: (i, 0)), pl.BlockSpec((1, d), lambda i: (0, 0))],
        out_specs=pl.BlockSpec((tm, d), lambda i: (i, 0)),
        out_shape=jax.ShapeDtypeStruct((n, d), out_dtype),
        compiler_params=_cparams(("parallel",)),
        name="rmsnorm",
    )(x, g.reshape(1, d))


def _mm_kernel(*refs, n_a, w_src, n_extra, n_out, nk, epilogue):
    n_w = len(w_src)
    a_refs = refs[:n_a]
    w_refs = refs[n_a:n_a + n_w]
    e_refs = refs[n_a + n_w:n_a + n_w + n_extra]
    o_refs = refs[n_a + n_w + n_extra:n_a + n_w + n_extra + n_out]
    acc_refs = refs[n_a + n_w + n_extra + n_out:]
    parts = [jnp.dot(a_refs[s][...], w[...], preferred_element_type=F32) for w, s in zip(w_refs, w_src)]
    if nk == 1:
        epilogue(parts, e_refs, o_refs)
        return
    k = pl.program_id(2)

    @pl.when(k == 0)
    def _():
        for acc, p in zip(acc_refs, parts):
            acc[...] = p

    @pl.when(k > 0)
    def _():
        for acc, p in zip(acc_refs, parts):
            acc[...] += p

    @pl.when(k == nk - 1)
    def _():
        epilogue([acc[...] for acc in acc_refs], e_refs, o_refs)


def matmul(a_list, w_list, w_src, extras, extra_specs, out_shapes, out_specs, epilogue, *, tm, tn, tk=None, name):
    m, kdim = a_list[0].shape
    n = w_list[0].shape[1]
    tk = kdim if tk is None else tk
    nk = kdim // tk
    assert m % tm == 0 and n % tn == 0 and kdim % tk == 0
    in_specs = [pl.BlockSpec((tm, tk), lambda i, j, k: (i, k)) for _ in a_list]
    in_specs += [pl.BlockSpec((tk, tn), lambda i, j, k: (k, j)) for _ in w_list]
    in_specs += list(extra_specs)
    scratch = [pltpu.VMEM((tm, tn), F32) for _ in w_list] if nk > 1 else []
    return pl.pallas_call(
        functools.partial(_mm_kernel, n_a=len(a_list), w_src=tuple(w_src), n_extra=len(extras),
                          n_out=len(out_shapes), nk=nk, epilogue=epilogue),
        grid=(m // tm, n // tn, nk),
        in_specs=in_specs,
        out_specs=list(out_specs),
        out_shape=list(out_shapes),
        scratch_shapes=scratch,
        compiler_params=_cparams(("parallel", "parallel", "arbitrary")),
        name=name,
    )(*a_list, *w_list, *extras)


def _tile_spec(tm, tn, col_block_offset=0):
    return pl.BlockSpec((tm, tn), lambda i, j, k: (i, j + col_block_offset))


def _row_spec(tm, width):
    return pl.BlockSpec((tm, width), lambda i, j, k: (i, 0))


def _ep_store(parts, e_refs, o_refs):
    o_refs[0][...] = parts[0].astype(o_refs[0].dtype)


def _ep_residual(parts, e_refs, o_refs):
    o_refs[0][...] = e_refs[0][...] + parts[0]


def _ep_swiglu(parts, e_refs, o_refs):
    o_refs[0][...] = (_silu(parts[0]) * parts[1]).astype(o_refs[0].dtype)


def _ep_gated_sum(parts, e_refs, o_refs):
    ga = e_refs[0][...].astype(F32)
    gb = e_refs[1][...].astype(F32)
    o_refs[0][...] = (_sigmoid(ga) * parts[0] + _sigmoid(gb) * parts[1]).astype(o_refs[0].dtype)


def _rope128(g, c, s1, s2):
    return g * c + pltpu.roll(g, 32, 1) * s1 + pltpu.roll(g, LANES - 32, 1) * s2


def _ep_mla_pre(parts, e_refs, o_refs, *, q_rank, kv_rank, eps):
    acc = parts[0]
    qg, kvg, c, s1, s2 = (r[...] for r in e_refs)
    cq = acc[:, :q_rank]
    ckv = acc[:, q_rank:q_rank + kv_rank]
    kp = acc[:, q_rank + kv_rank:q_rank + kv_rank + LANES]
    cq = cq * lax.rsqrt(jnp.mean(cq * cq, axis=-1, keepdims=True) + eps) * qg
    ckv = ckv * lax.rsqrt(jnp.mean(ckv * ckv, axis=-1, keepdims=True) + eps) * kvg
    o_refs[0][...] = cq.astype(o_refs[0].dtype)
    o_refs[1][...] = ckv.astype(o_refs[1].dtype)
    o_refs[2][...] = _rope128(kp, c, s1, s2).astype(o_refs[2].dtype)


def _ep_mla_q(parts, e_refs, o_refs, *, heads, scale):
    acc = parts[0]
    c, s1, s2 = (r[...] for r in e_refs)
    for h in range(heads):
        base = 2 * LANES * h
        o_refs[0][h, :, :LANES] = (acc[:, base:base + LANES] * scale).astype(o_refs[0].dtype)
        pe = _rope128(acc[:, base + LANES:base + 2 * LANES], c, s1, s2)
        o_refs[0][h, :, LANES:] = (pe * scale).astype(o_refs[0].dtype)


def _ep_mla_kv(parts, e_refs, o_refs, *, heads):
    acc = parts[0]
    kpe = e_refs[0][...]
    for h in range(heads):
        base = 2 * LANES * h
        o_refs[0][h, :, :LANES] = acc[:, base:base + LANES].astype(o_refs[0].dtype)
        o_refs[0][h, :, LANES:] = kpe
        o_refs[1][h] = acc[:, base + LANES:base + 2 * LANES].astype(o_refs[1].dtype)


ATTN_SPLIT = 2


def _attn_kernel(q_ref, k_ref, v_ref, o_ref, *, tq, chunk):
    qi = pl.program_id(2)
    dv = v_ref.shape[-1]
    rows = tq // ATTN_SPLIT
    qs = [q_ref[pl.ds(i * rows, rows), :] for i in range(ATTN_SPLIT)]
    r_chunk = lax.broadcasted_iota(jnp.int32, (rows, rows), 0) // chunk
    c_chunk = lax.broadcasted_iota(jnp.int32, (rows, rows), 1) // chunk
    diag_mask = c_chunk <= r_chunk

    def update(q, k, v, carry, mask):
        m, l, acc = carry
        s = lax.dot_general(q, k, (((1,), (1,)), ((), ())), preferred_element_type=F32)
        if mask is not None:
            s = jnp.where(mask, s, -jnp.inf)
        m_new = jnp.maximum(m, jnp.max(s, axis=1, keepdims=True))
        p = jnp.exp2(s - m_new[:, :1])
        alpha = jnp.exp2(m - m_new)
        l = alpha * l + jnp.sum(p, axis=1, keepdims=True)
        acc = alpha[:, :dv] * acc + jnp.dot(p.astype(BF16), v, preferred_element_type=F32)
        return m_new, l, acc

    def full_block(kb, carry):
        ks = pl.multiple_of(kb * tq, tq)
        k = k_ref[pl.ds(ks, tq), :]
        v = v_ref[pl.ds(ks, tq), :]
        return tuple(update(qs[i], k, v, carry[i], None) for i in range(ATTN_SPLIT))

    init = tuple((jnp.full((rows, LANES), -jnp.inf, F32), jnp.zeros((rows, LANES), F32), jnp.zeros((rows, dv), F32))
                 for _ in range(ATTN_SPLIT))
    carry = list(lax.fori_loop(0, qi, full_block, init))
    for i in range(ATTN_SPLIT):
        for d in range(i + 1):
            ks = pl.multiple_of(qi * tq + d * rows, rows)
            k = k_ref[pl.ds(ks, rows), :]
            v = v_ref[pl.ds(ks, rows), :]
            carry[i] = update(qs[i], k, v, carry[i], diag_mask if d == i else None)
    for i in range(ATTN_SPLIT):
        _, l, acc = carry[i]
        o_ref[pl.ds(i * rows, rows), :] = (acc / l[:, :dv]).astype(o_ref.dtype)


def attention(q_full, k_full, v, *, tq, chunk):
    b, h, s, dq = q_full.shape
    dv = v.shape[-1]
    assert s % tq == 0 and (tq // ATTN_SPLIT) % chunk == 0 and dv == LANES
    return pl.pallas_call(
        functools.partial(_attn_kernel, tq=tq, chunk=chunk),
        grid=(b, h, s // tq),
        in_specs=[
            pl.BlockSpec((None, None, tq, dq), lambda bi, hi, qi: (bi, hi, qi, 0)),
            pl.BlockSpec((None, None, s, dq), lambda bi, hi, qi: (bi, hi, 0, 0)),
            pl.BlockSpec((None, None, s, dv), lambda bi, hi, qi: (bi, hi, 0, 0)),
        ],
        out_specs=pl.BlockSpec((None, tq, dv), lambda bi, hi, qi: (bi, qi, hi)),
        out_shape=jax.ShapeDtypeStruct((b, s, h * dv), BF16),
        compiler_params=_cparams(("parallel", "parallel", "arbitrary")),
        name="mla_attention",
    )(q_full, k_full, v)


def _split3(x):
    hi = x.astype(BF16)
    r = x - hi.astype(F32)
    mid = r.astype(BF16)
    lo = (r - mid.astype(F32)).astype(BF16)
    return hi, mid, lo


def _hgrn_kernel(q_ref, f_ref, i_ref, og_ref, lb_ref, gain_ref, o_ref, state_ref, *, rows, eps):
    @pl.when(pl.program_id(2) == 0)
    def _():
        state_ref[...] = jnp.zeros_like(state_ref)

    dk = q_ref.shape[-1]
    dv = i_ref.shape[-1]
    lb = lb_ref[...]
    log_lb = jnp.log(lb)
    log_1m_lb = jnp.log1p(-lb)
    one_m_lb = 1.0 - lb
    gain = gain_ref[...]
    cl = HG_CHUNK
    halves = [1 << l for l in range(cl.bit_length() - 1)]
    chunks = [slice(c, c + cl) for c in range(0, rows, cl)]
    ones_k = jnp.ones((dk, LANES), BF16)
    ones_v = jnp.full((dv, LANES), 1.0 / dv, F32).astype(BF16)

    t_id = lax.broadcasted_iota(jnp.int32, (cl, cl), 0)
    s_id = lax.broadcasted_iota(jnp.int32, (cl, cl), 1)
    row = lax.broadcasted_iota(jnp.int32, (cl, dk), 0)
    cum_rows = [s_id <= t_id] + [s_id <= (t_id // (2 * h)) * (2 * h) + h for h in halves]
    cum_rows = jnp.concatenate([jnp.where(m, 1.0, 0.0) for m in cum_rows], axis=0).astype(BF16)
    upper = [(row % (2 * h)) >= h for h in halves]
    pair = [(t_id // (2 * h) == s_id // (2 * h)) & (t_id % (2 * h) >= h) & (s_id % (2 * h) < h) for h in halves]

    z = f_ref[...]
    y = log_1m_lb + (jnp.minimum(z, 0.0) - jnp.log1p(jnp.exp(-jnp.abs(z))))
    g = jnp.maximum(log_lb, y) + jnp.log1p(jnp.exp(-jnp.abs(log_lb - y)))
    kk = one_m_lb / (1.0 + jnp.exp(z))
    q = _silu(q_ref[...])
    v = i_ref[...]
    v16 = v.astype(BF16)
    g3 = jnp.concatenate(_split3(g), axis=1)

    cums = []
    for sl in chunks:
        c3 = jnp.dot(cum_rows, g3[sl], preferred_element_type=F32)
        cums.append(c3[:, :dk] + c3[:, dk:2 * dk] + c3[:, 2 * dk:])

    scores = []
    for sl, cum in zip(chunks, cums):
        b = cum[:cl]
        qc, kc = q[sl], kk[sl]
        acc = None
        for l, h in enumerate(halves):
            e = jnp.exp(-jnp.abs(b - cum[cl * (l + 1):cl * (l + 2)]))
            x = (jnp.where(upper[l], qc, kc) * e).astype(BF16)
            p = lax.dot_general(x, x, (((1,), (1,)), ((), ())), preferred_element_type=F32)
            p = jnp.where(pair[l], p, 0.0)
            acc = p if acc is None else acc + p
        scores.append(acc.astype(BF16))

    incs, decs, q_decs = [], [], []
    for sl, cum in zip(chunks, cums):
        b = cum[:cl]
        b_last = b[cl - 1:cl, :]
        k_dec = kk[sl] * jnp.exp(b_last - b)
        incs.append(lax.dot_general(v16[sl], k_dec.astype(BF16), (((0,), (0,)), ((), ())),
                                    preferred_element_type=F32))
        decs.append(jnp.exp(b_last))
        q_decs.append((q[sl] * jnp.exp(b)).astype(BF16))
    st = state_ref[...]
    o_inter = []
    for j in range(len(chunks)):
        o_inter.append(lax.dot_general(q_decs[j], st.astype(BF16), (((1,), (1,)), ((), ())),
                                       preferred_element_type=F32))
        st = st * decs[j] + incs[j]
    state_ref[...] = st

    outs = []
    for j, sl in enumerate(chunks):
        diag = jnp.dot((q[sl] * kk[sl]).astype(BF16), ones_k, preferred_element_type=F32)[:, :dv]
        outs.append(o_inter[j] + jnp.dot(scores[j], v16[sl], preferred_element_type=F32) + diag * v[sl])
    o = jnp.concatenate(outs, axis=0)
    ms = jnp.dot((o * o).astype(BF16), ones_v, preferred_element_type=F32)[:, :dv]
    o = o * lax.rsqrt(ms + eps) * gain * _silu(og_ref[...])
    o_ref[...] = o.astype(o_ref.dtype)


def hgrn2(hg, lb, gain, *, heads, dk, dv, rows, eps):
    b, s, _ = hg.shape
    assert dk == LANES and dv == LANES and s % rows == 0 and rows % HG_CHUNK == 0

    def col(group):
        return pl.BlockSpec((None, rows, dk), lambda bi, hi, si: (bi, si, group * heads + hi))

    vec = pl.BlockSpec((1, dk), lambda bi, hi, si: (0, hi))
    return pl.pallas_call(
        functools.partial(_hgrn_kernel, rows=rows, eps=eps),
        grid=(b, heads, s // rows),
        in_specs=[col(0), col(1), col(2), col(3), vec, vec],
        out_specs=pl.BlockSpec((None, rows, dv), lambda bi, hi, si: (bi, si, hi)),
        out_shape=jax.ShapeDtypeStruct((b, s, heads * dv), BF16),
        scratch_shapes=[pltpu.VMEM((dv, dk), F32)],
        compiler_params=_cparams(("parallel", "parallel", "arbitrary")),
        name="hgrn2",
    )(hg, hg, hg, hg, lb.reshape(1, -1), gain.reshape(1, -1))


META_E1, META_E2, META_R1, META_R2, META_W1, META_W2 = range(6)
HI16 = 0xFFFF0000


def _pack_bf16_pair(lo, hi):
    lo_bits = lax.bitcast_convert_type(lo.astype(jnp.bfloat16).astype(F32), jnp.uint32) >> 16
    hi_bits = lax.bitcast_convert_type(hi.astype(jnp.bfloat16).astype(F32), jnp.uint32)
    return hi_bits | lo_bits


def _unpack_bf16_pair(word):
    lo = lax.bitcast_convert_type(word << 16, F32).astype(BF16)
    hi = lax.bitcast_convert_type(word & jnp.uint32(HI16), F32).astype(BF16)
    return lo, hi


def _router_kernel(x_ref, g_ref, wr_ref, hp_ref, meta_ref, cnt_ref, base_ref, *, n_experts, eps):
    @pl.when(pl.program_id(0) == 0)
    def _():
        base_ref[...] = jnp.zeros_like(base_ref)

    tm, d = x_ref.shape
    half = d // 2
    x = x_ref[...]
    h = x * lax.rsqrt(jnp.mean(x * x, axis=-1, keepdims=True) + eps) * g_ref[...]
    hp_ref[...] = _pack_bf16_pair(h[:, :half], h[:, half:])

    logits = jnp.dot(h, wr_ref[...], preferred_element_type=F32, precision=lax.Precision.HIGHEST)
    lane = lax.broadcasted_iota(jnp.int32, logits.shape, 1)
    logits = jnp.where(lane < n_experts, logits, -jnp.inf)
    m1 = jnp.max(logits, axis=1, keepdims=True)
    i1 = jnp.min(jnp.where(logits == m1, lane, LANES), axis=1, keepdims=True)
    rest = jnp.where(lane == i1, -jnp.inf, logits)
    m2 = jnp.max(rest, axis=1, keepdims=True)
    i2 = jnp.min(jnp.where(rest == m2, lane, LANES), axis=1, keepdims=True)
    e2 = jnp.exp(m2 - m1)
    w1 = 1.0 / (1.0 + e2)
    w2 = e2 / (1.0 + e2)

    sel1 = lane == i1
    sel2 = lane == i2
    sel = jnp.where(sel1 | sel2, 1.0, 0.0)
    r_id = lax.broadcasted_iota(jnp.int32, (tm, tm), 0)
    c_id = lax.broadcasted_iota(jnp.int32, (tm, tm), 1)
    before = jnp.where(c_id < r_id, 1.0, 0.0).astype(BF16)
    rank = base_ref[...] + jnp.dot(before, sel.astype(BF16), preferred_element_type=F32)
    r1 = jnp.sum(jnp.where(sel1, rank, 0.0), axis=1, keepdims=True)
    r2 = jnp.sum(jnp.where(sel2, rank, 0.0), axis=1, keepdims=True)
    base_ref[...] += jnp.sum(sel, axis=0, keepdims=True)
    cnt_ref[...] = jnp.broadcast_to(base_ref[...], cnt_ref.shape)

    record = jnp.zeros(logits.shape, F32)
    for lane_id, val in ((META_E1, i1.astype(F32)), (META_E2, i2.astype(F32)), (META_R1, r1), (META_R2, r2),
                         (META_W1, w1), (META_W2, w2)):
        record = jnp.where(lane == lane_id, val, record)
    meta_ref[...] = record


def router(x, g, w_router, *, tm, eps):
    n, d = x.shape
    n_experts = w_router.shape[1]
    wr = jnp.zeros((d, LANES), F32).at[:, :n_experts].set(w_router)
    return pl.pallas_call(
        functools.partial(_router_kernel, n_experts=n_experts, eps=eps),
        grid=(n // tm,),
        in_specs=[pl.BlockSpec((tm, d), lambda i: (i, 0)), pl.BlockSpec((1, d), lambda i: (0, 0)),
                  pl.BlockSpec((d, LANES), lambda i: (0, 0))],
        out_specs=[pl.BlockSpec((tm, d // 2), lambda i: (i, 0)), pl.BlockSpec((tm, LANES), lambda i: (i, 0)),
                   pl.BlockSpec((8, LANES), lambda i: (0, 0))],
        out_shape=[jax.ShapeDtypeStruct((n, d // 2), jnp.uint32), jax.ShapeDtypeStruct((n, LANES), F32),
                   jax.ShapeDtypeStruct((8, LANES), F32)],
        scratch_shapes=[pltpu.VMEM((1, LANES), F32)],
        compiler_params=_cparams(("arbitrary",)),
        name="ffn_norm_router",
    )(x, g.reshape(1, d), wr)


def _row_copy(src_ref, src_row, dst_ref, dst_row, sem):
    return pltpu.make_async_copy(src_ref.at[pl.ds(src_row, 1), :], dst_ref.at[pl.ds(dst_row, 1), :], sem)


def _dispatch_kernel(dest_ref, h_ref, zeros_ref, xs_ref, sem, *, tokens):
    del zeros_ref
    base = pl.program_id(0) * tokens

    def issue(t, carry):
        for s in range(2):
            _row_copy(h_ref, t, xs_ref, dest_ref[2 * (base + t) + s], sem).start()
        return carry

    def drain(t, carry):
        for s in range(2):
            _row_copy(h_ref, 0, xs_ref, 0, sem).wait()
        return carry

    lax.fori_loop(0, tokens, issue, 0)
    lax.fori_loop(0, tokens, drain, 0)


def dispatch(dest, hp, rows_padded, *, tokens):
    n, words = hp.shape
    zeros = jnp.zeros((rows_padded, words), hp.dtype)
    return pl.pallas_call(
        functools.partial(_dispatch_kernel, tokens=tokens),
        grid_spec=pltpu.PrefetchScalarGridSpec(
            num_scalar_prefetch=1, grid=(n // tokens,),
            in_specs=[pl.BlockSpec((tokens, words), lambda i, dest: (i, 0)),
                      pl.BlockSpec(memory_space=pl.ANY)],
            out_specs=pl.BlockSpec(memory_space=pl.ANY),
            scratch_shapes=[pltpu.SemaphoreType.DMA(())]),
        out_shape=jax.ShapeDtypeStruct(zeros.shape, zeros.dtype),
        input_output_aliases={2: 0},
        compiler_params=_cparams(("arbitrary",)),
        name="moe_dispatch",
    )(dest, hp, zeros)


def _expert_up_kernel(te_ref, nu_ref, x_ref, w1_ref, w3_ref, u_ref, xb_ref):
    i, j = pl.program_id(0), pl.program_id(1)
    used = i < nu_ref[0]
    half = x_ref.shape[1]

    @pl.when(used & (j == 0))
    def _():
        lo, hi = _unpack_bf16_pair(x_ref[...])
        xb_ref[:, :half] = lo
        xb_ref[:, half:] = hi

    @pl.when(used)
    def _():
        xb = xb_ref[...]
        a1 = jnp.dot(xb, w1_ref[...], preferred_element_type=F32)
        a3 = jnp.dot(xb, w3_ref[...], preferred_element_type=F32)
        u_ref[...] = (_silu(a1) * a3).astype(u_ref.dtype)

    @pl.when(jnp.logical_not(used))
    def _():
        u_ref[...] = jnp.zeros_like(u_ref)


def _expert_down_kernel(te_ref, nu_ref, u_ref, w2_ref, y_ref):
    used = pl.program_id(0) < nu_ref[0]
    half = w2_ref.shape[1] // 2

    @pl.when(used)
    def _():
        acc = jnp.dot(u_ref[...], w2_ref[...], preferred_element_type=F32)
        y_ref[...] = _pack_bf16_pair(acc[:, :half], acc[:, half:])

    @pl.when(jnp.logical_not(used))
    def _():
        y_ref[...] = jnp.zeros_like(y_ref)


def expert_ffn(tile_expert, n_used, xs, w1, w3, w2, *, tm, tn):
    n_exp, d, dff = w1.shape
    rows = xs.shape[0]
    n_tiles = rows // tm

    def tile(i, nu):
        return jnp.minimum(i, nu[0] - 1)

    def col(i, j, nu, n_col):
        return jnp.where(i < nu[0], j, n_col - 1)

    u = pl.pallas_call(
        _expert_up_kernel,
        grid_spec=pltpu.PrefetchScalarGridSpec(
            num_scalar_prefetch=2, grid=(n_tiles, dff // tn),
            in_specs=[pl.BlockSpec((tm, d // 2), lambda i, j, te, nu: (tile(i, nu), 0)),
                      pl.BlockSpec((None, d, tn), lambda i, j, te, nu: (te[i], 0, col(i, j, nu, dff // tn))),
                      pl.BlockSpec((None, d, tn), lambda i, j, te, nu: (te[i], 0, col(i, j, nu, dff // tn)))],
            out_specs=pl.BlockSpec((tm, tn), lambda i, j, te, nu: (i, j)),
            scratch_shapes=[pltpu.VMEM((tm, d), BF16)]),
        out_shape=jax.ShapeDtypeStruct((rows, dff), BF16),
        compiler_params=_cparams(("arbitrary", "arbitrary")),
        name="moe_expert_up",
    )(tile_expert, n_used, xs, w1, w3)
    return pl.pallas_call(
        _expert_down_kernel,
        grid_spec=pltpu.PrefetchScalarGridSpec(
            num_scalar_prefetch=2, grid=(n_tiles, d // tn),
            in_specs=[pl.BlockSpec((tm, dff), lambda i, j, te, nu: (tile(i, nu), 0)),
                      pl.BlockSpec((None, dff, tn), lambda i, j, te, nu: (te[i], 0, col(i, j, nu, d // tn)))],
            out_specs=pl.BlockSpec((tm, tn // 2), lambda i, j, te, nu: (i, j))),
        out_shape=jax.ShapeDtypeStruct((rows, d // 2), jnp.uint32),
        compiler_params=_cparams(("arbitrary", "arbitrary")),
        name="moe_expert_down",
    )(tile_expert, n_used, u, w2)


def _combine_kernel(dest_ref, x_ref, meta_ref, g_ref, y_ref, o_ref, buf_ref, sem, *, tokens, pair_block, eps,
                    final_norm):
    i = pl.program_id(0)
    slot = i % 2
    words = buf_ref.shape[-1]

    def fetch(step, into):
        def issue(t, carry):
            for s in range(2):
                _row_copy(y_ref, dest_ref[2 * (step * tokens + t) + s], buf_ref.at[into, s], t, sem.at[into]).start()
            return carry

        lax.fori_loop(0, tokens, issue, 0)

    @pl.when(i == 0)
    def _():
        fetch(0, 0)

    @pl.when(i + 1 < pl.num_programs(0))
    def _():
        fetch(i + 1, 1 - slot)

    def drain(t, carry):
        for s in range(2):
            _row_copy(y_ref, 0, buf_ref.at[slot, s], 0, sem.at[slot]).wait()
        return carry

    lax.fori_loop(0, tokens, drain, 0)

    meta = meta_ref[...]
    w1 = meta[:, META_W1:META_W1 + 1]
    w2 = meta[:, META_W2:META_W2 + 1]
    w1 = jnp.broadcast_to(w1, (tokens, LANES))
    w2 = jnp.broadcast_to(w2, (tokens, LANES))
    sq = jnp.zeros((tokens, LANES), F32)
    half = pair_block // 2
    chunks = half // LANES
    for c in range(words // LANES):
        lo_col = (c // chunks) * pair_block + (c % chunks) * LANES
        y1 = buf_ref[slot, 0, :, c * LANES:(c + 1) * LANES]
        y2 = buf_ref[slot, 1, :, c * LANES:(c + 1) * LANES]
        for col, shift in ((lo_col, True), (lo_col + half, False)):
            cols = slice(col, col + LANES)
            a1 = lax.bitcast_convert_type(y1 << 16 if shift else y1 & jnp.uint32(HI16), F32)
            a2 = lax.bitcast_convert_type(y2 << 16 if shift else y2 & jnp.uint32(HI16), F32)
            o = x_ref[:, cols] + w1 * a1 + w2 * a2
            o_ref[:, cols] = o
            sq = sq + o * o
    if final_norm:
        d = 2 * words
        inv = jnp.broadcast_to(lax.rsqrt(jnp.sum(sq, axis=1, keepdims=True) / d + eps), (tokens, LANES))
        for c in range(d // LANES):
            cols = slice(c * LANES, (c + 1) * LANES)
            o_ref[:, cols] = o_ref[:, cols] * inv * g_ref[:, cols]


def combine(dest, x, meta, gain, y, *, tokens, pair_block, eps, final_norm):
    n, d = x.shape
    return pl.pallas_call(
        functools.partial(_combine_kernel, tokens=tokens, pair_block=pair_block, eps=eps, final_norm=final_norm),
        grid_spec=pltpu.PrefetchScalarGridSpec(
            num_scalar_prefetch=1, grid=(n // tokens,),
            in_specs=[pl.BlockSpec((tokens, d), lambda i, dest: (i, 0)),
                      pl.BlockSpec((tokens, LANES), lambda i, dest: (i, 0)),
                      pl.BlockSpec((1, d), lambda i, dest: (0, 0)),
                      pl.BlockSpec(memory_space=pl.ANY)],
            out_specs=pl.BlockSpec((tokens, d), lambda i, dest: (i, 0)),
            scratch_shapes=[pltpu.VMEM((2, 2, tokens, d // 2), jnp.uint32), pltpu.SemaphoreType.DMA((2,))]),
        out_shape=jax.ShapeDtypeStruct((n, d), F32),
        compiler_params=_cparams(("arbitrary",)),
        name="moe_combine",
    )(dest, x, meta, gain.reshape(1, d), y)


def _rope_tables(positions, dm):
    half = dm.mla_rope // 2
    inv_freq = dm.rope_theta ** (-jnp.arange(half, dtype=F32) / half)
    ang = positions.astype(F32).reshape(-1, 1) * inv_freq
    cos, sin = jnp.cos(ang), jnp.sin(ang)
    zero = jnp.zeros_like(cos)
    c = jnp.concatenate([cos, cos, zero, zero], axis=-1)
    s1 = jnp.concatenate([zero, sin, zero, zero], axis=-1)
    s2 = jnp.concatenate([-sin, zero, zero, zero], axis=-1)
    return c, s1, s2


def _mixer(x, lp, tables, dm):
    n = dm.batch * dm.seq
    d = dm.d_model
    hg_cols = 4 * dm.hg_heads * dm.hg_dk
    c, s1, s2 = tables
    h = rmsnorm(x, lp["norm_mix"], BF16, tm=dm.tm_norm, eps=dm.eps)

    (hg,) = matmul([h], [lp["w_hg"]], [0], [], [], [jax.ShapeDtypeStruct((n, hg_cols), F32)],
                   [_tile_spec(dm.tm, dm.tn)], _ep_store, tm=dm.tm, tn=dm.tn, name="proj_hgrn")
    (gates,) = matmul([h], [lp["w_gates"]], [0], [], [], [jax.ShapeDtypeStruct((n, 2 * d), BF16)],
                      [_tile_spec(dm.tm, dm.tn)], _ep_store, tm=dm.tm, tn=dm.tn, name="proj_gates")

    mla_cols = lp["w_mla"].shape[1]
    tms = dm.tm_small
    cq, ckv, kpe = matmul(
        [h], [lp["w_mla"]], [0],
        [lp["q_gain"], lp["kv_gain"], c, s1, s2],
        [pl.BlockSpec((1, dm.mla_q_rank), lambda i, j, k: (0, 0)),
         pl.BlockSpec((1, dm.mla_kv_rank), lambda i, j, k: (0, 0)),
         _row_spec(tms, LANES), _row_spec(tms, LANES), _row_spec(tms, LANES)],
        [jax.ShapeDtypeStruct((n, dm.mla_q_rank), BF16), jax.ShapeDtypeStruct((n, dm.mla_kv_rank), BF16),
         jax.ShapeDtypeStruct((n, LANES), BF16)],
        [_row_spec(tms, dm.mla_q_rank), _row_spec(tms, dm.mla_kv_rank), _row_spec(tms, LANES)],
        functools.partial(_ep_mla_pre, q_rank=dm.mla_q_rank, kv_rank=dm.mla_kv_rank, eps=dm.eps),
        tm=tms, tn=mla_cols, name="proj_mla_latents")

    heads = dm.mla_heads
    hb = min(4, heads)
    tmu = dm.tm_up
    s_tiles = dm.seq // tmu

    def head_spec(width):
        return pl.BlockSpec((None, hb, tmu, width), lambda i, j, k: (i // s_tiles, j, i % s_tiles, 0))

    scale = (dm.mla_nope + dm.mla_rope) ** -0.5 * LOG2_E
    (q_full,) = matmul(
        [cq], [lp["w_uq"]], [0], [c, s1, s2],
        [_row_spec(tmu, LANES), _row_spec(tmu, LANES), _row_spec(tmu, LANES)],
        [jax.ShapeDtypeStruct((dm.batch, heads, dm.seq, 2 * LANES), BF16)], [head_spec(2 * LANES)],
        functools.partial(_ep_mla_q, heads=hb, scale=scale), tm=tmu, tn=hb * 2 * LANES, name="mla_q_up")
    k_full, v = matmul(
        [ckv], [lp["w_ukv"]], [0], [kpe], [_row_spec(tmu, LANES)],
        [jax.ShapeDtypeStruct((dm.batch, heads, dm.seq, 2 * LANES), BF16),
         jax.ShapeDtypeStruct((dm.batch, heads, dm.seq, dm.mla_dv), BF16)],
        [head_spec(2 * LANES), head_spec(dm.mla_dv)],
        functools.partial(_ep_mla_kv, heads=hb), tm=tmu, tn=hb * 2 * LANES, name="mla_kv_up")
    o_b = attention(q_full, k_full, v, tq=dm.tq, chunk=dm.chunk).reshape(n, heads * dm.mla_dv)

    o_a = hgrn2(hg.reshape(dm.batch, dm.seq, hg_cols), lp["lb"], lp["hg_gain"], heads=dm.hg_heads, dk=dm.hg_dk,
                dv=dm.hg_dv, rows=dm.hgrn_rows, eps=dm.eps).reshape(n, dm.hg_heads * dm.hg_dv)

    (y,) = matmul([o_a, o_b], [lp["w_branch_a"], lp["w_branch_b"]], [0, 1], [gates, gates],
                  [_tile_spec(dm.tm, dm.tn2), _tile_spec(dm.tm, dm.tn2, d // dm.tn2)],
                  [jax.ShapeDtypeStruct((n, d), BF16)], [_tile_spec(dm.tm, dm.tn2)], _ep_gated_sum,
                  tm=dm.tm, tn=dm.tn2, name="branch_merge")
    (x,) = matmul([y], [lp["w_out"]], [0], [x], [_tile_spec(dm.tm, dm.tn2)],
                  [jax.ShapeDtypeStruct((n, d), F32)], [_tile_spec(dm.tm, dm.tn2)], _ep_residual,
                  tm=dm.tm, tn=dm.tn2, name="mixer_out")
    return x


def _dense_ffn(x, lp, dm):
    n, d = x.shape
    h = rmsnorm(x, lp["norm_ffn"], BF16, tm=dm.tm_norm, eps=dm.eps)
    (u,) = matmul([h], [lp["w1"], lp["w3"]], [0, 0], [], [], [jax.ShapeDtypeStruct((n, dm.d_ff_pad), BF16)],
                  [_tile_spec(dm.tm, dm.tn2)], _ep_swiglu, tm=dm.tm, tn=dm.tn2, name="ffn_up")
    (x,) = matmul([u], [lp["w2"]], [0], [x], [_tile_spec(dm.tm, dm.tn)], [jax.ShapeDtypeStruct((n, d), F32)],
                  [_tile_spec(dm.tm, dm.tn)], _ep_residual, tm=dm.tm, tn=dm.tn, tk=dm.tk_ffn, name="ffn_down")
    return x


def _moe_ffn(x, lp, dm, final_gain):
    n, d = x.shape
    n_exp, tme = dm.n_experts, dm.tm_expert
    hp, meta, cnt = router(x, lp["norm_ffn"], lp["w_router"], tm=dm.tm_norm, eps=dm.eps)
    counts = cnt[0, :n_exp].astype(jnp.int32)
    padded = (counts + tme - 1) // tme * tme
    ends = jnp.cumsum(padded)
    starts = ends - padded
    e12 = meta[:, META_E1:META_E2 + 1].astype(jnp.int32)
    r12 = meta[:, META_R1:META_R2 + 1].astype(jnp.int32)
    dest = (starts[e12] + r12).reshape(-1)
    n_tiles = (2 * n) // tme + n_exp
    n_used = (ends[-1] // tme).reshape(1)
    tile_row = jnp.minimum(jnp.arange(n_tiles, dtype=jnp.int32), n_used - 1) * tme
    tile_expert = jnp.minimum(jnp.sum(tile_row[:, None] >= ends[None, :], axis=1), n_exp - 1).astype(jnp.int32)

    xs = dispatch(dest, hp, n_tiles * tme, tokens=dm.tm_small)
    y = expert_ffn(tile_expert, n_used, xs, lp["w1"], lp["w3"], lp["w2"], tm=tme, tn=dm.tn2)
    gain = jnp.ones((d,), F32) if final_gain is None else final_gain
    return combine(dest, x, meta, gain, y, tokens=dm.tm_combine, pair_block=dm.tn2, eps=dm.eps,
                   final_norm=final_gain is not None)


def _prepare_layer(l, p, lbs, dm):
    d = dm.d_model
    hg_cols = 4 * dm.hg_heads * dm.hg_dk
    mla_in = dm.mla_q_rank + dm.mla_kv_rank + dm.mla_rope
    w_in = p["w_in"][l]
    w_mla = jnp.pad(w_in[:, hg_cols:hg_cols + mla_in], ((0, 0), (0, LANES - dm.mla_rope)))
    heads = dm.mla_heads
    w_uq = p["w_uq"][l].reshape(dm.mla_q_rank, heads, dm.mla_nope + dm.mla_rope)
    w_uq = jnp.pad(w_uq, ((0, 0), (0, 0), (0, 2 * LANES - dm.mla_nope - dm.mla_rope)))
    hg_width = dm.hg_heads * dm.hg_dv
    lp = {
        "norm_mix": p["norm_mix"][l],
        "w_hg": w_in[:, :hg_cols].astype(BF16),
        "w_mla": w_mla.astype(BF16),
        "w_gates": w_in[:, hg_cols + mla_in:].astype(BF16),
        "lb": lbs[l],
        "hg_gain": p["hg_norm"][l],
        "q_gain": p["mla_q_norm"][l].reshape(1, -1),
        "kv_gain": p["mla_kv_norm"][l].reshape(1, -1),
        "w_uq": w_uq.reshape(dm.mla_q_rank, heads * 2 * LANES).astype(BF16),
        "w_ukv": p["w_ukv"][l].astype(BF16),
        "w_branch_a": p["w_branch"][l, :hg_width].astype(BF16),
        "w_branch_b": p["w_branch"][l, hg_width:].astype(BF16),
        "w_out": p["w_out"][l].astype(BF16),
        "norm_ffn": p["norm_ffn"][l],
    }
    if l % 2 == 0:
        pad = dm.d_ff_pad - dm.d_ff
        lp["w1"] = jnp.pad(p["ffn_w1"][l // 2], ((0, 0), (0, pad))).astype(BF16)
        lp["w3"] = jnp.pad(p["ffn_w3"][l // 2], ((0, 0), (0, pad))).astype(BF16)
        lp["w2"] = jnp.pad(p["ffn_w2"][l // 2], ((0, pad), (0, 0))).astype(BF16)
    else:
        lp["w_router"] = p["w_router"][l // 2]
        lp["w1"] = p["moe_w1"][l // 2].astype(BF16)
        lp["w3"] = p["moe_w3"][l // 2].astype(BF16)
        lp["w2"] = p["moe_w2"][l // 2].astype(BF16)
    return lp


def forward(p, dm):
    n = dm.batch * dm.seq
    x = p["x"].reshape(n, dm.d_model)
    tables = _rope_tables(p["positions"], dm)
    lbs = jnp.cumsum(jax.nn.softmax(p["hg_lb_logits"].astype(F32), axis=0), axis=0)
    lbs = lbs - lbs[0:1]
    for l in range(dm.depth):
        lp = _prepare_layer(l, p, lbs, dm)
        x = _mixer(x, lp, tables, dm)
        last = l == dm.depth - 1
        if l % 2 == 0:
            x = _dense_ffn(x, lp, dm)
            if last:
                x = rmsnorm(x, p["norm_final"], F32, tm=dm.tm_norm, eps=dm.eps)
        else:
            x = _moe_ffn(x, lp, dm, p["norm_final"] if last else None)
    return x.reshape(dm.batch, dm.seq, dm.d_model)


def kernel(x, positions, norm_mix, w_in, hg_lb_logits, hg_norm, mla_q_norm, w_uq, mla_kv_norm, w_ukv, w_branch, w_out, norm_ffn, ffn_w1, ffn_w3, ffn_w2, w_router, moe_w1, moe_w3, moe_w2, norm_final):
    p = dict(x=x, positions=positions, norm_mix=norm_mix, w_in=w_in, hg_lb_logits=hg_lb_logits, hg_norm=hg_norm,
             mla_q_norm=mla_q_norm, w_uq=w_uq, mla_kv_norm=mla_kv_norm, w_ukv=w_ukv, w_branch=w_branch, w_out=w_out,
             norm_ffn=norm_ffn, ffn_w1=ffn_w1, ffn_w3=ffn_w3, ffn_w2=ffn_w2, w_router=w_router, moe_w1=moe_w1,
             moe_w3=moe_w3, moe_w2=moe_w2, norm_final=norm_final)
    return forward(p, Dims())
```

```python
import functools
from typing import NamedTuple

import jax
import jax.numpy as jnp
from jax import lax
from jax.experimental import pallas as pl
from jax.experimental.pallas import tpu as pltpu

F32 = jnp.float32
BF16 = jnp.bfloat16

LANES = 128
V7X_VMEM_BYTES = 64 * 1024 * 1024
VMEM_LIMIT_BYTES = V7X_VMEM_BYTES - 8 * 1024 * 1024


class Dims(NamedTuple):
    d_model: int = 4096
    batch: int = 2
    seq: int = 8192
    depth: int = 2
    chunk: int = 64
    eps: float = 1e-6
    hg_dk: int = 128
    hg_heads: int = 16
    hg_dv: int = 128
    mla_dv: int = 128
    mla_heads: int = 16
    mla_nope: int = 128
    mla_rope: int = 64
    mla_q_rank: int = 768
    mla_kv_rank: int = 512
    rope_theta: float = 10000.0
    d_ff: int = 11008
    n_experts: int = 8
    d_ff_expert: int = 4096
    tm: int = 1024
    tn: int = 1024
    tn2: int = 512
    tk_ffn: int = 2816
    d_ff_pad: int = 11264
    tm_small: int = 512
    tm_norm: int = 256
    tm_expert: int = 512
    tm_combine: int = 256
    tm_up: int = 1024
    tq: int = 2048
    hgrn_rows: int = 1024


LOG2_E = 1.4426950408889634
HG_CHUNK = 64


def _cparams(semantics):
    return pltpu.CompilerParams(dimension_semantics=semantics, vmem_limit_bytes=VMEM_LIMIT_BYTES)


def _sigmoid(x):
    return 1.0 / (1.0 + jnp.exp(-x))


def _silu(x):
    return x * _sigmoid(x)


def _rmsnorm_kernel(x_ref, g_ref, o_ref, *, eps):
    x = x_ref[...]
    ms = jnp.mean(x * x, axis=-1, keepdims=True)
    o_ref[...] = (x * lax.rsqrt(ms + eps) * g_ref[...]).astype(o_ref.dtype)


def rmsnorm(x, g, out_dtype, *, tm, eps):
    n, d = x.shape
    return pl.pallas_call(
        functools.partial(_rmsnorm_kernel, eps=eps),
        grid=(n // tm,),
        in_specs=[pl.BlockSpec((tm, d), lambda i: (i, 0)), pl.BlockSpec((1, d), lambda i: (0, 0))],
        out_specs=pl.BlockSpec((tm, d), lambda i: (i, 0)),
        out_shape=jax.ShapeDtypeStruct((n, d), out_dtype),
        compiler_params=_cparams(("parallel",)),
        name="rmsnorm",
    )(x, g.reshape(1, d))


def _mm_kernel(*refs, n_a, w_src, n_extra, n_out, nk, epilogue):
    n_w = len(w_src)
    a_refs = refs[:n_a]
    w_refs = refs[n_a:n_a + n_w]
    e_refs = refs[n_a + n_w:n_a + n_w + n_extra]
    o_refs = refs[n_a + n_w + n_extra:n_a + n_w + n_extra + n_out]
    acc_refs = refs[n_a + n_w + n_extra + n_out:]
    parts = [jnp.dot(a_refs[s][...], w[...], preferred_element_type=F32) for w, s in zip(w_refs, w_src)]
    if nk == 1:
        epilogue(parts, e_refs, o_refs)
        return
    k = pl.program_id(2)

    @pl.when(k == 0)
    def _():
        for acc, p in zip(acc_refs, parts):
            acc[...] = p

    @pl.when(k > 0)
    def _():
        for acc, p in zip(acc_refs, parts):
            acc[...] += p

    @pl.when(k == nk - 1)
    def _():
        epilogue([acc[...] for acc in acc_refs], e_refs, o_refs)


def matmul(a_list, w_list, w_src, extras, extra_specs, out_shapes, out_specs, epilogue, *, tm, tn, tk=None, name):
    m, kdim = a_list[0].shape
    n = w_list[0].shape[1]
    tk = kdim if tk is None else tk
    nk = kdim // tk
    assert m % tm == 0 and n % tn == 0 and kdim % tk == 0
    in_specs = [pl.BlockSpec((tm, tk), lambda i, j, k: (i, k)) for _ in a_list]
    in_specs += [pl.BlockSpec((tk, tn), lambda i, j, k: (k, j)) for _ in w_list]
    in_specs += list(extra_specs)
    scratch = [pltpu.VMEM((tm, tn), F32) for _ in w_list] if nk > 1 else []
    return pl.pallas_call(
        functools.partial(_mm_kernel, n_a=len(a_list), w_src=tuple(w_src), n_extra=len(extras),
                          n_out=len(out_shapes), nk=nk, epilogue=epilogue),
        grid=(m // tm, n // tn, nk),
        in_specs=in_specs,
        out_specs=list(out_specs),
        out_shape=list(out_shapes),
        scratch_shapes=scratch,
        compiler_params=_cparams(("parallel", "parallel", "arbitrary")),
        name=name,
    )(*a_list, *w_list, *extras)


def _tile_spec(tm, tn, col_block_offset=0):
    return pl.BlockSpec((tm, tn), lambda i, j, k: (i, j + col_block_offset))


def _row_spec(tm, width):
    return pl.BlockSpec((tm, width), lambda i, j, k: (i, 0))


def _ep_store(parts, e_refs, o_refs):
    o_refs[0][...] = parts[0].astype(o_refs[0].dtype)


def _ep_residual(parts, e_refs, o_refs):
    o_refs[0][...] = e_refs[0][...] + parts[0]


def _ep_swiglu(parts, e_refs, o_refs):
    o_refs[0][...] = (_silu(parts[0]) * parts[1]).astype(o_refs[0].dtype)


def _ep_gated_sum(parts, e_refs, o_refs):
    ga = e_refs[0][...].astype(F32)
    gb = e_refs[1][...].astype(F32)
    o_refs[0][...] = (_sigmoid(ga) * parts[0] + _sigmoid(gb) * parts[1]).astype(o_refs[0].dtype)


def _rope128(g, c, s1, s2):
    return g * c + pltpu.roll(g, 32, 1) * s1 + pltpu.roll(g, LANES - 32, 1) * s2


def _ep_mla_pre(parts, e_refs, o_refs, *, q_rank, kv_rank, eps):
    acc = parts[0]
    qg, kvg, c, s1, s2 = (r[...] for r in e_refs)
    cq = acc[:, :q_rank]
    ckv = acc[:, q_rank:q_rank + kv_rank]
    kp = acc[:, q_rank + kv_rank:q_rank + kv_rank + LANES]
    cq = cq * lax.rsqrt(jnp.mean(cq * cq, axis=-1, keepdims=True) + eps) * qg
    ckv = ckv * lax.rsqrt(jnp.mean(ckv * ckv, axis=-1, keepdims=True) + eps) * kvg
    o_refs[0][...] = cq.astype(o_refs[0].dtype)
    o_refs[1][...] = ckv.astype(o_refs[1].dtype)
    o_refs[2][...] = _rope128(kp, c, s1, s2).astype(o_refs[2].dtype)


def _ep_mla_q(parts, e_refs, o_refs, *, heads, scale):
    acc = parts[0]
    c, s1, s2 = (r[...] for r in e_refs)
    for h in range(heads):
        base = 2 * LANES * h
        o_refs[0][h, :, :LANES] = (acc[:, base:base + LANES] * scale).astype(o_refs[0].dtype)
        pe = _rope128(acc[:, base + LANES:base + 2 * LANES], c, s1, s2)
        o_refs[0][h, :, LANES:] = (pe * scale).astype(o_refs[0].dtype)


def _ep_mla_kv(parts, e_refs, o_refs, *, heads):
    acc = parts[0]
    kpe = e_refs[0][...]
    for h in range(heads):
        base = 2 * LANES * h
        o_refs[0][h, :, :LANES] = acc[:, base:base + LANES].astype(o_refs[0].dtype)
        o_refs[0][h, :, LANES:] = kpe
        o_refs[1][h] = acc[:, base + LANES:base + 2 * LANES].astype(o_refs[1].dtype)


ATTN_SPLIT = 2


def _attn_kernel(q_ref, k_ref, v_ref, o_ref, *, tq, chunk):
    qi = pl.program_id(2)
    dv = v_ref.shape[-1]
    rows = tq // ATTN_SPLIT
    qs = [q_ref[pl.ds(i * rows, rows), :] for i in range(ATTN_SPLIT)]
    r_chunk = lax.broadcasted_iota(jnp.int32, (rows, rows), 0) // chunk
    c_chunk = lax.broadcasted_iota(jnp.int32, (rows, rows), 1) // chunk
    diag_mask = c_chunk <= r_chunk

    def update(q, k, v, carry, mask):
        m, l, acc = carry
        s = lax.dot_general(q, k, (((1,), (1,)), ((), ())), preferred_element_type=F32)
        if mask is not None:
            s = jnp.where(mask, s, -jnp.inf)
        m_new = jnp.maximum(m, jnp.max(s, axis=1, keepdims=True))
        p = jnp.exp2(s - m_new[:, :1])
        alpha = jnp.exp2(m - m_new)
        l = alpha * l + jnp.sum(p, axis=1, keepdims=True)
        acc = alpha[:, :dv] * acc + jnp.dot(p.astype(BF16), v, preferred_element_type=F32)
        return m_new, l, acc

    def full_block(kb, carry):
        ks = pl.multiple_of(kb * tq, tq)
        k = k_ref[pl.ds(ks, tq), :]
        v = v_ref[pl.ds(ks, tq), :]
        return tuple(update(qs[i], k, v, carry[i], None) for i in range(ATTN_SPLIT))

    init = tuple((jnp.full((rows, LANES), -jnp.inf, F32), jnp.zeros((rows, LANES), F32), jnp.zeros((rows, dv), F32))
                 for _ in range(ATTN_SPLIT))
    carry = list(lax.fori_loop(0, qi, full_block, init))
    for i in range(ATTN_SPLIT):
        for d in range(i + 1):
            ks = pl.multiple_of(qi * tq + d * rows, rows)
            k = k_ref[pl.ds(ks, rows), :]
            v = v_ref[pl.ds(ks, rows), :]
            carry[i] = update(qs[i], k, v, carry[i], diag_mask if d == i else None)
    for i in range(ATTN_SPLIT):
        _, l, acc = carry[i]
        o_ref[pl.ds(i * rows, rows), :] = (acc / l[:, :dv]).astype(o_ref.dtype)


def attention(q_full, k_full, v, *, tq, chunk):
    b, h, s, dq = q_full.shape
    dv = v.shape[-1]
    assert s % tq == 0 and (tq // ATTN_SPLIT) % chunk == 0 and dv == LANES
    return pl.pallas_call(
        functools.partial(_attn_kernel, tq=tq, chunk=chunk),
        grid=(b, h, s // tq),
        in_specs=[
            pl.BlockSpec((None, None, tq, dq), lambda bi, hi, qi: (bi, hi, qi, 0)),
            pl.BlockSpec((None, None, s, dq), lambda bi, hi, qi: (bi, hi, 0, 0)),
            pl.BlockSpec((None, None, s, dv), lambda bi, hi, qi: (bi, hi, 0, 0)),
        ],
        out_specs=pl.BlockSpec((None, tq, dv), lambda bi, hi, qi: (bi, qi, hi)),
        out_shape=jax.ShapeDtypeStruct((b, s, h * dv), BF16),
        compiler_params=_cparams(("parallel", "parallel", "arbitrary")),
        name="mla_attention",
    )(q_full, k_full, v)


def _split3(x):
    hi = x.astype(BF16)
    r = x - hi.astype(F32)
    mid = r.astype(BF16)
    lo = (r - mid.astype(F32)).astype(BF16)
    return hi, mid, lo


def _hgrn_kernel(q_ref, f_ref, i_ref, og_ref, lb_ref, gain_ref, o_ref, state_ref, *, rows, eps):
    @pl.when(pl.program_id(2) == 0)
    def _():
        state_ref[...] = jnp.zeros_like(state_ref)

    dk = q_ref.shape[-1]
    dv = i_ref.shape[-1]
    lb = lb_ref[...]
    log_lb = jnp.log(lb)
    log_1m_lb = jnp.log1p(-lb)
    one_m_lb = 1.0 - lb
    gain = gain_ref[...]
    cl = HG_CHUNK
    halves = [1 << l for l in range(cl.bit_length() - 1)]
    chunks = [slice(c, c + cl) for c in range(0, rows, cl)]
    ones_k = jnp.ones((dk, LANES), BF16)
    ones_v = jnp.full((dv, LANES), 1.0 / dv, F32).astype(BF16)

    t_id = lax.broadcasted_iota(jnp.int32, (cl, cl), 0)
    s_id = lax.broadcasted_iota(jnp.int32, (cl, cl), 1)
    row = lax.broadcasted_iota(jnp.int32, (cl, dk), 0)
    cum_rows = [s_id <= t_id] + [s_id <= (t_id // (2 * h)) * (2 * h) + h for h in halves]
    cum_rows = jnp.concatenate([jnp.where(m, 1.0, 0.0) for m in cum_rows], axis=0).astype(BF16)
    upper = [(row % (2 * h)) >= h for h in halves]
    pair = [(t_id // (2 * h) == s_id // (2 * h)) & (t_id % (2 * h) >= h) & (s_id % (2 * h) < h) for h in halves]

    z = f_ref[...]
    y = log_1m_lb + (jnp.minimum(z, 0.0) - jnp.log1p(jnp.exp(-jnp.abs(z))))
    g = jnp.maximum(log_lb, y) + jnp.log1p(jnp.exp(-jnp.abs(log_lb - y)))
    kk = one_m_lb / (1.0 + jnp.exp(z))
    q = _silu(q_ref[...])
    v = i_ref[...]
    v16 = v.astype(BF16)
    g3 = jnp.concatenate(_split3(g), axis=1)

    cums = []
    for sl in chunks:
        c3 = jnp.dot(cum_rows, g3[sl], preferred_element_type=F32)
        cums.append(c3[:, :dk] + c3[:, dk:2 * dk] + c3[:, 2 * dk:])

    scores = []
    for sl, cum in zip(chunks, cums):
        b = cum[:cl]
        qc, kc = q[sl], kk[sl]
        acc = None
        for l, h in enumerate(halves):
            e = jnp.exp(-jnp.abs(b - cum[cl * (l + 1):cl * (l + 2)]))
            x = (jnp.where(upper[l], qc, kc) * e).astype(BF16)
            p = lax.dot_general(x, x, (((1,), (1,)), ((), ())), preferred_element_type=F32)
            p = jnp.where(pair[l], p, 0.0)
            acc = p if acc is None else acc + p
        scores.append(acc.astype(BF16))

    incs, decs, q_decs = [], [], []
    for sl, cum in zip(chunks, cums):
        b = cum[:cl]
        b_last = b[cl - 1:cl, :]
        k_dec = kk[sl] * jnp.exp(b_last - b)
        incs.append(lax.dot_general(v16[sl], k_dec.astype(BF16), (((0,), (0,)), ((), ())),
                                    preferred_element_type=F32))
        decs.append(jnp.exp(b_last))
        q_decs.append((q[sl] * jnp.exp(b)).astype(BF16))
    st = state_ref[...]
    o_inter = []
    for j in range(len(chunks)):
        o_inter.append(lax.dot_general(q_decs[j], st.astype(BF16), (((1,), (1,)), ((), ())),
                                       preferred_element_type=F32))
        st = st * decs[j] + incs[j]
    state_ref[...] = st

    outs = []
    for j, sl in enumerate(chunks):
        diag = jnp.dot((q[sl] * kk[sl]).astype(BF16), ones_k, preferred_element_type=F32)[:, :dv]
        outs.append(o_inter[j] + jnp.dot(scores[j], v16[sl], preferred_element_type=F32) + diag * v[sl])
    o = jnp.concatenate(outs, axis=0)
    ms = jnp.dot((o * o).astype(BF16), ones_v, preferred_element_type=F32)[:, :dv]
    o = o * lax.rsqrt(ms + eps) * gain * _silu(og_ref[...])
    o_ref[...] = o.astype(o_ref.dtype)


def hgrn2(hg, lb, gain, *, heads, dk, dv, rows, eps):
    b, s, _ = hg.shape
    assert dk == LANES and dv == LANES and s % rows == 0 and rows % HG_CHUNK == 0

    def col(group):
        return pl.BlockSpec((None, rows, dk), lambda bi, hi, si: (bi, si, group * heads + hi))

    vec = pl.BlockSpec((1, dk), lambda bi, hi, si: (0, hi))
    return pl.pallas_call(
        functools.partial(_hgrn_kernel, rows=rows, eps=eps),
        grid=(b, heads, s // rows),
        in_specs=[col(0), col(1), col(2), col(3), vec, vec],
        out_specs=pl.BlockSpec((None, rows, dv), lambda bi, hi, si: (bi, si, hi)),
        out_shape=jax.ShapeDtypeStruct((b, s, heads * dv), BF16),
        scratch_shapes=[pltpu.VMEM((dv, dk), F32)],
        compiler_params=_cparams(("parallel", "parallel", "arbitrary")),
        name="hgrn2",
    )(hg, hg, hg, hg, lb.reshape(1, -1), gain.reshape(1, -1))


META_E1, META_E2, META_R1, META_R2, META_W1, META_W2 = range(6)
HI16 = 0xFFFF0000


def _pack_bf16_pair(lo, hi):
    lo_bits = lax.bitcast_convert_type(lo.astype(jnp.bfloat16).astype(F32), jnp.uint32) >> 16
    hi_bits = lax.bitcast_convert_type(hi.astype(jnp.bfloat16).astype(F32), jnp.uint32)
    return hi_bits | lo_bits


def _unpack_bf16_pair(word):
    lo = lax.bitcast_convert_type(word << 16, F32).astype(BF16)
    hi = lax.bitcast_convert_type(word & jnp.uint32(HI16), F32).astype(BF16)
    return lo, hi


def _router_kernel(x_ref, g_ref, wr_ref, hp_ref, meta_ref, cnt_ref, base_ref, *, n_experts, eps):
    @pl.when(pl.program_id(0) == 0)
    def _():
        base_ref[...] = jnp.zeros_like(base_ref)

    tm, d = x_ref.shape
    half = d // 2
    x = x_ref[...]
    h = x * lax.rsqrt(jnp.mean(x * x, axis=-1, keepdims=True) + eps) * g_ref[...]
    hp_ref[...] = _pack_bf16_pair(h[:, :half], h[:, half:])

    h_hi = h.astype(BF16)
    h_lo = (h - h_hi.astype(F32)).astype(BF16)
    wr = wr_ref[...]
    w_hi = wr.astype(BF16)
    w_lo = (wr - w_hi.astype(F32)).astype(BF16)
    logits = (jnp.dot(h_hi, w_hi, preferred_element_type=F32)
              + (jnp.dot(h_lo, w_hi, preferred_element_type=F32) + jnp.dot(h_hi, w_lo, preferred_element_type=F32)))
    lane = lax.broadcasted_iota(jnp.int32, logits.shape, 1)
    logits = jnp.where(lane < n_experts, logits, -jnp.inf)
    m1 = jnp.max(logits, axis=1, keepdims=True)
    i1 = jnp.min(jnp.where(logits == m1, lane, LANES), axis=1, keepdims=True)
    rest = jnp.where(lane == i1, -jnp.inf, logits)
    m2 = jnp.max(rest, axis=1, keepdims=True)
    i2 = jnp.min(jnp.where(rest == m2, lane, LANES), axis=1, keepdims=True)
    e2 = jnp.exp(m2 - m1)
    w1 = 1.0 / (1.0 + e2)
    w2 = e2 / (1.0 + e2)

    sel1 = lane == i1
    sel2 = lane == i2
    sel = jnp.where(sel1 | sel2, 1.0, 0.0)
    r_id = lax.broadcasted_iota(jnp.int32, (tm, tm), 0)
    c_id = lax.broadcasted_iota(jnp.int32, (tm, tm), 1)
    before = jnp.where(c_id < r_id, 1.0, 0.0).astype(BF16)
    rank = base_ref[...] + jnp.dot(before, sel.astype(BF16), preferred_element_type=F32)
    r1 = jnp.sum(jnp.where(sel1, rank, 0.0), axis=1, keepdims=True)
    r2 = jnp.sum(jnp.where(sel2, rank, 0.0), axis=1, keepdims=True)
    base_ref[...] += jnp.sum(sel, axis=0, keepdims=True)
    cnt_ref[...] = jnp.broadcast_to(base_ref[...], cnt_ref.shape)

    record = jnp.zeros(logits.shape, F32)
    for lane_id, val in ((META_E1, i1.astype(F32)), (META_E2, i2.astype(F32)), (META_R1, r1), (META_R2, r2),
                         (META_W1, w1), (META_W2, w2)):
        record = jnp.where(lane == lane_id, val, record)
    meta_ref[...] = record


def router(x, g, w_router, *, tm, eps):
    n, d = x.shape
    n_experts = w_router.shape[1]
    wr = jnp.zeros((d, LANES), F32).at[:, :n_experts].set(w_router)
    return pl.pallas_call(
        functools.partial(_router_kernel, n_experts=n_experts, eps=eps),
        grid=(n // tm,),
        in_specs=[pl.BlockSpec((tm, d), lambda i: (i, 0)), pl.BlockSpec((1, d), lambda i: (0, 0)),
                  pl.BlockSpec((d, LANES), lambda i: (0, 0))],
        out_specs=[pl.BlockSpec((tm, d // 2), lambda i: (i, 0)), pl.BlockSpec((tm, LANES), lambda i: (i, 0)),
                   pl.BlockSpec((8, LANES), lambda i: (0, 0))],
        out_shape=[jax.ShapeDtypeStruct((n, d // 2), jnp.uint32), jax.ShapeDtypeStruct((n, LANES), F32),
                   jax.ShapeDtypeStruct((8, LANES), F32)],
        scratch_shapes=[pltpu.VMEM((1, LANES), F32)],
        compiler_params=_cparams(("arbitrary",)),
        name="ffn_norm_router",
    )(x, g.reshape(1, d), wr)


def _row_copy(src_ref, src_row, dst_ref, dst_row, sem):
    return pltpu.make_async_copy(src_ref.at[pl.ds(src_row, 1), :], dst_ref.at[pl.ds(dst_row, 1), :], sem)


def _dispatch_kernel(dest_ref, h_ref, zeros_ref, xs_ref, sem, *, tokens):
    del zeros_ref
    base = pl.program_id(0) * tokens

    def issue(t, carry):
        for s in range(2):
            _row_copy(h_ref, t, xs_ref, dest_ref[2 * (base + t) + s], sem).start()
        return carry

    def drain(t, carry):
        for s in range(2):
            _row_copy(h_ref, 0, xs_ref, 0, sem).wait()
        return carry

    lax.fori_loop(0, tokens, issue, 0)
    lax.fori_loop(0, tokens, drain, 0)


def dispatch(dest, hp, rows_padded, *, tokens):
    n, words = hp.shape
    zeros = jnp.zeros((rows_padded, words), hp.dtype)
    return pl.pallas_call(
        functools.partial(_dispatch_kernel, tokens=tokens),
        grid_spec=pltpu.PrefetchScalarGridSpec(
            num_scalar_prefetch=1, grid=(n // tokens,),
            in_specs=[pl.BlockSpec((tokens, words), lambda i, dest: (i, 0)),
                      pl.BlockSpec(memory_space=pl.ANY)],
            out_specs=pl.BlockSpec(memory_space=pl.ANY),
            scratch_shapes=[pltpu.SemaphoreType.DMA(())]),
        out_shape=jax.ShapeDtypeStruct(zeros.shape, zeros.dtype),
        input_output_aliases={2: 0},
        compiler_params=_cparams(("arbitrary",)),
        name="moe_dispatch",
    )(dest, hp, zeros)


def _expert_up_kernel(te_ref, nu_ref, x_ref, w1_ref, w3_ref, u_ref, xb_ref):
    i, j = pl.program_id(0), pl.program_id(1)
    used = i < nu_ref[0]
    half = x_ref.shape[1]

    @pl.when(used & (j == 0))
    def _():
        lo, hi = _unpack_bf16_pair(x_ref[...])
        xb_ref[:, :half] = lo
        xb_ref[:, half:] = hi

    @pl.when(used)
    def _():
        xb = xb_ref[...]
        a1 = jnp.dot(xb, w1_ref[...], preferred_element_type=F32)
        a3 = jnp.dot(xb, w3_ref[...], preferred_element_type=F32)
        u_ref[...] = (_silu(a1) * a3).astype(u_ref.dtype)

    @pl.when(jnp.logical_not(used))
    def _():
        u_ref[...] = jnp.zeros_like(u_ref)


def _expert_down_kernel(te_ref, nu_ref, u_ref, w2_ref, y_ref):
    used = pl.program_id(0) < nu_ref[0]
    half = w2_ref.shape[1] // 2

    @pl.when(used)
    def _():
        acc = jnp.dot(u_ref[...], w2_ref[...].astype(BF16), preferred_element_type=F32)
        y_ref[...] = _pack_bf16_pair(acc[:, :half], acc[:, half:])

    @pl.when(jnp.logical_not(used))
    def _():
        y_ref[...] = jnp.zeros_like(y_ref)


def expert_ffn(tile_expert, n_used, xs, w1, w3, w2, *, tm, tn):
    n_exp, d, dff = w1.shape
    rows = xs.shape[0]
    n_tiles = rows // tm

    def tile(i, nu):
        return jnp.minimum(i, nu[0] - 1)

    def col(i, j, nu, n_col):
        return jnp.where(i < nu[0], j, n_col - 1)

    u = pl.pallas_call(
        _expert_up_kernel,
        grid_spec=pltpu.PrefetchScalarGridSpec(
            num_scalar_prefetch=2, grid=(n_tiles, dff // tn),
            in_specs=[pl.BlockSpec((tm, d // 2), lambda i, j, te, nu: (tile(i, nu), 0)),
                      pl.BlockSpec((None, d, tn), lambda i, j, te, nu: (te[i], 0, col(i, j, nu, dff // tn))),
                      pl.BlockSpec((None, d, tn), lambda i, j, te, nu: (te[i], 0, col(i, j, nu, dff // tn)))],
            out_specs=pl.BlockSpec((tm, tn), lambda i, j, te, nu: (i, j)),
            scratch_shapes=[pltpu.VMEM((tm, d), BF16)]),
        out_shape=jax.ShapeDtypeStruct((rows, dff), BF16),
        compiler_params=_cparams(("arbitrary", "arbitrary")),
        name="moe_expert_up",
    )(tile_expert, n_used, xs, w1, w3)
    return pl.pallas_call(
        _expert_down_kernel,
        grid_spec=pltpu.PrefetchScalarGridSpec(
            num_scalar_prefetch=2, grid=(n_tiles, d // tn),
            in_specs=[pl.BlockSpec((tm, dff), lambda i, j, te, nu: (tile(i, nu), 0)),
                      pl.BlockSpec((None, dff, tn), lambda i, j, te, nu: (te[i], 0, col(i, j, nu, d // tn)))],
            out_specs=pl.BlockSpec((tm, tn // 2), lambda i, j, te, nu: (i, j))),
        out_shape=jax.ShapeDtypeStruct((rows, d // 2), jnp.uint32),
        compiler_params=_cparams(("arbitrary", "arbitrary")),
        name="moe_expert_down",
    )(tile_expert, n_used, u, w2)


def _combine_kernel(dest_ref, x_ref, meta_ref, g_ref, y_ref, o_ref, buf_ref, sem, *, tokens, pair_block, eps,
                    final_norm):
    i = pl.program_id(0)
    slot = i % 2
    words = buf_ref.shape[-1]

    def fetch(step, into):
        def issue(t, carry):
            for s in range(2):
                _row_copy(y_ref, dest_ref[2 * (step * tokens + t) + s], buf_ref.at[into, s], t, sem.at[into]).start()
            return carry

        lax.fori_loop(0, tokens, issue, 0)

    @pl.when(i == 0)
    def _():
        fetch(0, 0)

    @pl.when(i + 1 < pl.num_programs(0))
    def _():
        fetch(i + 1, 1 - slot)

    def drain(t, carry):
        for s in range(2):
            _row_copy(y_ref, 0, buf_ref.at[slot, s], 0, sem.at[slot]).wait()
        return carry

    lax.fori_loop(0, tokens, drain, 0)

    meta = meta_ref[...]
    w1 = meta[:, META_W1:META_W1 + 1]
    w2 = meta[:, META_W2:META_W2 + 1]
    w1 = jnp.broadcast_to(w1, (tokens, LANES))
    w2 = jnp.broadcast_to(w2, (tokens, LANES))
    sq = jnp.zeros((tokens, LANES), F32)
    half = pair_block // 2
    chunks = half // LANES
    for c in range(words // LANES):
        lo_col = (c // chunks) * pair_block + (c % chunks) * LANES
        y1 = buf_ref[slot, 0, :, c * LANES:(c + 1) * LANES]
        y2 = buf_ref[slot, 1, :, c * LANES:(c + 1) * LANES]
        for col, shift in ((lo_col, True), (lo_col + half, False)):
            cols = slice(col, col + LANES)
            a1 = lax.bitcast_convert_type(y1 << 16 if shift else y1 & jnp.uint32(HI16), F32)
            a2 = lax.bitcast_convert_type(y2 << 16 if shift else y2 & jnp.uint32(HI16), F32)
            o = x_ref[:, cols] + w1 * a1 + w2 * a2
            o_ref[:, cols] = o
            sq = sq + o * o
    if final_norm:
        d = 2 * words
        inv = jnp.broadcast_to(lax.rsqrt(jnp.sum(sq, axis=1, keepdims=True) / d + eps), (tokens, LANES))
        for c in range(d // LANES):
            cols = slice(c * LANES, (c + 1) * LANES)
            o_ref[:, cols] = o_ref[:, cols] * inv * g_ref[:, cols]


def combine(dest, x, meta, gain, y, *, tokens, pair_block, eps, final_norm):
    n, d = x.shape
    return pl.pallas_call(
        functools.partial(_combine_kernel, tokens=tokens, pair_block=pair_block, eps=eps, final_norm=final_norm),
        grid_spec=pltpu.PrefetchScalarGridSpec(
            num_scalar_prefetch=1, grid=(n // tokens,),
            in_specs=[pl.BlockSpec((tokens, d), lambda i, dest: (i, 0)),
                      pl.BlockSpec((tokens, LANES), lambda i, dest: (i, 0)),
                      pl.BlockSpec((1, d), lambda i, dest: (0, 0)),
                      pl.BlockSpec(memory_space=pl.ANY)],
            out_specs=pl.BlockSpec((tokens, d), lambda i, dest: (i, 0)),
            scratch_shapes=[pltpu.VMEM((2, 2, tokens, d // 2), jnp.uint32), pltpu.SemaphoreType.DMA((2,))]),
        out_shape=jax.ShapeDtypeStruct((n, d), F32),
        compiler_params=_cparams(("arbitrary",)),
        name="moe_combine",
    )(dest, x, meta, gain.reshape(1, d), y)


def _rope_tables(positions, dm):
    half = dm.mla_rope // 2
    inv_freq = dm.rope_theta ** (-jnp.arange(half, dtype=F32) / half)
    ang = positions.astype(F32).reshape(-1, 1) * inv_freq
    cos, sin = jnp.cos(ang), jnp.sin(ang)
    zero = jnp.zeros_like(cos)
    c = jnp.concatenate([cos, cos, zero, zero], axis=-1)
    s1 = jnp.concatenate([zero, sin, zero, zero], axis=-1)
    s2 = jnp.concatenate([-sin, zero, zero, zero], axis=-1)
    return c, s1, s2


def _mixer(x, lp, tables, dm):
    n = dm.batch * dm.seq
    d = dm.d_model
    hg_cols = 4 * dm.hg_heads * dm.hg_dk
    c, s1, s2 = tables
    h = rmsnorm(x, lp["norm_mix"], BF16, tm=dm.tm_norm, eps=dm.eps)

    (hg,) = matmul([h], [lp["w_hg"]], [0], [], [], [jax.ShapeDtypeStruct((n, hg_cols), F32)],
                   [_tile_spec(dm.tm, dm.tn)], _ep_store, tm=dm.tm, tn=dm.tn, name="proj_hgrn")
    (gates,) = matmul([h], [lp["w_gates"]], [0], [], [], [jax.ShapeDtypeStruct((n, 2 * d), BF16)],
                      [_tile_spec(dm.tm, dm.tn)], _ep_store, tm=dm.tm, tn=dm.tn, name="proj_gates")

    mla_cols = lp["w_mla"].shape[1]
    tms = dm.tm_small
    cq, ckv, kpe = matmul(
        [h], [lp["w_mla"]], [0],
        [lp["q_gain"], lp["kv_gain"], c, s1, s2],
        [pl.BlockSpec((1, dm.mla_q_rank), lambda i, j, k: (0, 0)),
         pl.BlockSpec((1, dm.mla_kv_rank), lambda i, j, k: (0, 0)),
         _row_spec(tms, LANES), _row_spec(tms, LANES), _row_spec(tms, LANES)],
        [jax.ShapeDtypeStruct((n, dm.mla_q_rank), BF16), jax.ShapeDtypeStruct((n, dm.mla_kv_rank), BF16),
         jax.ShapeDtypeStruct((n, LANES), BF16)],
        [_row_spec(tms, dm.mla_q_rank), _row_spec(tms, dm.mla_kv_rank), _row_spec(tms, LANES)],
        functools.partial(_ep_mla_pre, q_rank=dm.mla_q_rank, kv_rank=dm.mla_kv_rank, eps=dm.eps),
        tm=tms, tn=mla_cols, name="proj_mla_latents")

    heads = dm.mla_heads
    hb = min(4, heads)
    tmu = dm.tm_up
    s_tiles = dm.seq // tmu

    def head_spec(width):
        return pl.BlockSpec((None, hb, tmu, width), lambda i, j, k: (i // s_tiles, j, i % s_tiles, 0))

    scale = (dm.mla_nope + dm.mla_rope) ** -0.5 * LOG2_E
    (q_full,) = matmul(
        [cq], [lp["w_uq"]], [0], [c, s1, s2],
        [_row_spec(tmu, LANES), _row_spec(tmu, LANES), _row_spec(tmu, LANES)],
        [jax.ShapeDtypeStruct((dm.batch, heads, dm.seq, 2 * LANES), BF16)], [head_spec(2 * LANES)],
        functools.partial(_ep_mla_q, heads=hb, scale=scale), tm=tmu, tn=hb * 2 * LANES, name="mla_q_up")
    k_full, v = matmul(
        [ckv], [lp["w_ukv"]], [0], [kpe], [_row_spec(tmu, LANES)],
        [jax.ShapeDtypeStruct((dm.batch, heads, dm.seq, 2 * LANES), BF16),
         jax.ShapeDtypeStruct((dm.batch, heads, dm.seq, dm.mla_dv), BF16)],
        [head_spec(2 * LANES), head_spec(dm.mla_dv)],
        functools.partial(_ep_mla_kv, heads=hb), tm=tmu, tn=hb * 2 * LANES, name="mla_kv_up")
    o_b = attention(q_full, k_full, v, tq=dm.tq, chunk=dm.chunk).reshape(n, heads * dm.mla_dv)

    o_a = hgrn2(hg.reshape(dm.batch, dm.seq, hg_cols), lp["lb"], lp["hg_gain"], heads=dm.hg_heads, dk=dm.hg_dk,
                dv=dm.hg_dv, rows=dm.hgrn_rows, eps=dm.eps).reshape(n, dm.hg_heads * dm.hg_dv)

    (y,) = matmul([o_a, o_b], [lp["w_branch_a"], lp["w_branch_b"]], [0, 1], [gates, gates],
                  [_tile_spec(dm.tm, dm.tn2), _tile_spec(dm.tm, dm.tn2, d // dm.tn2)],
                  [jax.ShapeDtypeStruct((n, d), BF16)], [_tile_spec(dm.tm, dm.tn2)], _ep_gated_sum,
                  tm=dm.tm, tn=dm.tn2, name="branch_merge")
    (x,) = matmul([y], [lp["w_out"]], [0], [x], [_tile_spec(dm.tm, dm.tn2)],
                  [jax.ShapeDtypeStruct((n, d), F32)], [_tile_spec(dm.tm, dm.tn2)], _ep_residual,
                  tm=dm.tm, tn=dm.tn2, name="mixer_out")
    return x


def _dense_ffn(x, lp, dm):
    n, d = x.shape
    h = rmsnorm(x, lp["norm_ffn"], BF16, tm=dm.tm_norm, eps=dm.eps)
    (u,) = matmul([h], [lp["w1"], lp["w3"]], [0, 0], [], [], [jax.ShapeDtypeStruct((n, dm.d_ff_pad), BF16)],
                  [_tile_spec(dm.tm, dm.tn2)], _ep_swiglu, tm=dm.tm, tn=dm.tn2, name="ffn_up")
    (x,) = matmul([u], [lp["w2"]], [0], [x], [_tile_spec(dm.tm, dm.tn)], [jax.ShapeDtypeStruct((n, d), F32)],
                  [_tile_spec(dm.tm, dm.tn)], _ep_residual, tm=dm.tm, tn=dm.tn, tk=dm.tk_ffn, name="ffn_down")
    return x


def _moe_ffn(x, lp, dm, final_gain):
    n, d = x.shape
    n_exp, tme = dm.n_experts, dm.tm_expert
    hp, meta, cnt = router(x, lp["norm_ffn"], lp["w_router"], tm=dm.tm_norm, eps=dm.eps)
    counts = cnt[0, :n_exp].astype(jnp.int32)
    padded = (counts + tme - 1) // tme * tme
    ends = jnp.cumsum(padded)
    starts = ends - padded
    e12 = meta[:, META_E1:META_E2 + 1].astype(jnp.int32)
    r12 = meta[:, META_R1:META_R2 + 1].astype(jnp.int32)
    dest = (starts[e12] + r12).reshape(-1)
    n_tiles = (2 * n) // tme + n_exp
    n_used = (ends[-1] // tme).reshape(1)
    tile_row = jnp.minimum(jnp.arange(n_tiles, dtype=jnp.int32), n_used - 1) * tme
    tile_expert = jnp.minimum(jnp.sum(tile_row[:, None] >= ends[None, :], axis=1), n_exp - 1).astype(jnp.int32)

    xs = dispatch(dest, hp, n_tiles * tme, tokens=dm.tm_small)
    y = expert_ffn(tile_expert, n_used, xs, lp["w1"], lp["w3"], lp["w2"], tm=tme, tn=dm.tn2)
    gain = jnp.ones((d,), F32) if final_gain is None else final_gain
    return combine(dest, x, meta, gain, y, tokens=dm.tm_combine, pair_block=dm.tn2, eps=dm.eps,
                   final_norm=final_gain is not None)


def _prepare_layer(l, p, lbs, dm):
    d = dm.d_model
    hg_cols = 4 * dm.hg_heads * dm.hg_dk
    mla_in = dm.mla_q_rank + dm.mla_kv_rank + dm.mla_rope
    w_in = p["w_in"][l]
    w_mla = jnp.pad(w_in[:, hg_cols:hg_cols + mla_in], ((0, 0), (0, LANES - dm.mla_rope)))
    heads = dm.mla_heads
    w_uq = p["w_uq"][l].reshape(dm.mla_q_rank, heads, dm.mla_nope + dm.mla_rope)
    w_uq = jnp.pad(w_uq, ((0, 0), (0, 0), (0, 2 * LANES - dm.mla_nope - dm.mla_rope)))
    hg_width = dm.hg_heads * dm.hg_dv
    lp = {
        "norm_mix": p["norm_mix"][l],
        "w_hg": w_in[:, :hg_cols].astype(BF16),
        "w_mla": w_mla.astype(BF16),
        "w_gates": w_in[:, hg_cols + mla_in:].astype(BF16),
        "lb": lbs[l],
        "hg_gain": p["hg_norm"][l],
        "q_gain": p["mla_q_norm"][l].reshape(1, -1),
        "kv_gain": p["mla_kv_norm"][l].reshape(1, -1),
        "w_uq": w_uq.reshape(dm.mla_q_rank, heads * 2 * LANES).astype(BF16),
        "w_ukv": p["w_ukv"][l].astype(BF16),
        "w_branch_a": p["w_branch"][l, :hg_width].astype(BF16),
        "w_branch_b": p["w_branch"][l, hg_width:].astype(BF16),
        "w_out": p["w_out"][l].astype(BF16),
        "norm_ffn": p["norm_ffn"][l],
    }
    if l % 2 == 0:
        pad = dm.d_ff_pad - dm.d_ff
        lp["w1"] = jnp.pad(p["ffn_w1"][l // 2], ((0, 0), (0, pad))).astype(BF16)
        lp["w3"] = jnp.pad(p["ffn_w3"][l // 2], ((0, 0), (0, pad))).astype(BF16)
        lp["w2"] = jnp.pad(p["ffn_w2"][l // 2], ((0, pad), (0, 0))).astype(BF16)
    else:
        lp["w_router"] = p["w_router"][l // 2]
        lp["w1"] = p["moe_w1"][l // 2].astype(BF16)
        lp["w3"] = p["moe_w3"][l // 2].astype(BF16)
        lp["w2"] = p["moe_w2"][l // 2]
    return lp


def forward(p, dm):
    n = dm.batch * dm.seq
    x = p["x"].reshape(n, dm.d_model)
    tables = _rope_tables(p["positions"], dm)
    lbs = jnp.cumsum(jax.nn.softmax(p["hg_lb_logits"].astype(F32), axis=0), axis=0)
    lbs = lbs - lbs[0:1]
    for l in range(dm.depth):
        lp = _prepare_layer(l, p, lbs, dm)
        x = _mixer(x, lp, tables, dm)
        last = l == dm.depth - 1
        if l % 2 == 0:
            x = _dense_ffn(x, lp, dm)
            if last:
                x = rmsnorm(x, p["norm_final"], F32, tm=dm.tm_norm, eps=dm.eps)
        else:
            x = _moe_ffn(x, lp, dm, p["norm_final"] if last else None)
    return x.reshape(dm.batch, dm.seq, dm.d_model)


def kernel(x, positions, norm_mix, w_in, hg_lb_logits, hg_norm, mla_q_norm, w_uq, mla_kv_norm, w_ukv, w_branch, w_out, norm_ffn, ffn_w1, ffn_w3, ffn_w2, w_router, moe_w1, moe_w3, moe_w2, norm_final):
    p = dict(x=x, positions=positions, norm_mix=norm_mix, w_in=w_in, hg_lb_logits=hg_lb_logits, hg_norm=hg_norm,
             mla_q_norm=mla_q_norm, w_uq=w_uq, mla_kv_norm=mla_kv_norm, w_ukv=w_ukv, w_branch=w_branch, w_out=w_out,
             norm_ffn=norm_ffn, ffn_w1=ffn_w1, ffn_w3=ffn_w3, ffn_w2=ffn_w2, w_router=w_router, moe_w1=moe_w1,
             moe_w3=moe_w3, moe_w2=moe_w2, norm_final=norm_final)
    return forward(p, Dims())
```

```python
import functools
from typing import NamedTuple

import jax
import jax.numpy as jnp
from jax import lax
from jax.experimental import pallas as pl
from jax.experimental.pallas import tpu as pltpu

F32 = jnp.float32
BF16 = jnp.bfloat16

LANES = 128
V7X_VMEM_BYTES = 64 * 1024 * 1024
VMEM_LIMIT_BYTES = V7X_VMEM_BYTES - 8 * 1024 * 1024


class Dims(NamedTuple):
    d_model: int = 4096
    batch: int = 2
    seq: int = 8192
    depth: int = 2
    chunk: int = 64
    eps: float = 1e-6
    hg_dk: int = 128
    hg_heads: int = 16
    hg_dv: int = 128
    mla_dv: int = 128
    mla_heads: int = 16
    mla_nope: int = 128
    mla_rope: int = 64
    mla_q_rank: int = 768
    mla_kv_rank: int = 512
    rope_theta: float = 10000.0
    d_ff: int = 11008
    n_experts: int = 8
    d_ff_expert: int = 4096
    tm: int = 1024
    tn: int = 1024
    tn2: int = 512
    tk_ffn: int = 2816
    d_ff_pad: int = 11264
    tm_small: int = 512
    tm_norm: int = 256
    tm_expert: int = 512
    tm_combine: int = 256
    tm_up: int = 1024
    tq: int = 2048
    hgrn_rows: int = 2048


LOG2_E = 1.4426950408889634
HG_CHUNK = 64


def _cparams(semantics):
    return pltpu.CompilerParams(dimension_semantics=semantics, vmem_limit_bytes=VMEM_LIMIT_BYTES)


def _sigmoid(x):
    return 1.0 / (1.0 + jnp.exp(-x))


def _silu(x):
    return x * _sigmoid(x)


def _rmsnorm_kernel(x_ref, g_ref, o_ref, *, eps):
    x = x_ref[...]
    ms = jnp.mean(x * x, axis=-1, keepdims=True)
    o_ref[...] = (x * lax.rsqrt(ms + eps) * g_ref[...]).astype(o_ref.dtype)


def rmsnorm(x, g, out_dtype, *, tm, eps):
    n, d = x.shape
    return pl.pallas_call(
        functools.partial(_rmsnorm_kernel, eps=eps),
        grid=(n // tm,),
        in_specs=[pl.BlockSpec((tm, d), lambda i: (i, 0)), pl.BlockSpec((1, d), lambda i: (0, 0))],
        out_specs=pl.BlockSpec((tm, d), lambda i: (i, 0)),
        out_shape=jax.ShapeDtypeStruct((n, d), out_dtype),
        compiler_params=_cparams(("parallel",)),
        name="rmsnorm",
    )(x, g.reshape(1, d))


def _mm_kernel(*refs, n_a, w_src, n_extra, n_out, nk, epilogue):
    n_w = len(w_src)
    a_refs = refs[:n_a]
    w_refs = refs[n_a:n_a + n_w]
    e_refs = refs[n_a + n_w:n_a + n_w + n_extra]
    o_refs = refs[n_a + n_w + n_extra:n_a + n_w + n_extra + n_out]
    acc_refs = refs[n_a + n_w + n_extra + n_out:]
    parts = [jnp.dot(a_refs[s][...], w[...], preferred_element_type=F32) for w, s in zip(w_refs, w_src)]
    if nk == 1:
        epilogue(parts, e_refs, o_refs)
        return
    k = pl.program_id(2)

    @pl.when(k == 0)
    def _():
        for acc, p in zip(acc_refs, parts):
            acc[...] = p

    @pl.when(k > 0)
    def _():
        for acc, p in zip(acc_refs, parts):
            acc[...] += p

    @pl.when(k == nk - 1)
    def _():
        epilogue([acc[...] for acc in acc_refs], e_refs, o_refs)


def matmul(a_list, w_list, w_src, extras, extra_specs, out_shapes, out_specs, epilogue, *, tm, tn, tk=None, name):
    m, kdim = a_list[0].shape
    n = w_list[0].shape[1]
    tk = kdim if tk is None else tk
    nk = kdim // tk
    assert m % tm == 0 and n % tn == 0 and kdim % tk == 0
    in_specs = [pl.BlockSpec((tm, tk), lambda i, j, k: (i, k)) for _ in a_list]
    in_specs += [pl.BlockSpec((tk, tn), lambda i, j, k: (k, j)) for _ in w_list]
    in_specs += list(extra_specs)
    scratch = [pltpu.VMEM((tm, tn), F32) for _ in w_list] if nk > 1 else []
    return pl.pallas_call(
        functools.partial(_mm_kernel, n_a=len(a_list), w_src=tuple(w_src), n_extra=len(extras),
                          n_out=len(out_shapes), nk=nk, epilogue=epilogue),
        grid=(m // tm, n // tn, nk),
        in_specs=in_specs,
        out_specs=list(out_specs),
        out_shape=list(out_shapes),
        scratch_shapes=scratch,
        compiler_params=_cparams(("parallel", "parallel", "arbitrary")),
        name=name,
    )(*a_list, *w_list, *extras)


def _tile_spec(tm, tn, col_block_offset=0):
    return pl.BlockSpec((tm, tn), lambda i, j, k: (i, j + col_block_offset))


def _row_spec(tm, width):
    return pl.BlockSpec((tm, width), lambda i, j, k: (i, 0))


def _ep_store(parts, e_refs, o_refs):
    o_refs[0][...] = parts[0].astype(o_refs[0].dtype)


def _ep_residual(parts, e_refs, o_refs):
    o_refs[0][...] = e_refs[0][...] + parts[0]


def _ep_swiglu(parts, e_refs, o_refs):
    o_refs[0][...] = (_silu(parts[0]) * parts[1]).astype(o_refs[0].dtype)


def _ep_gated_sum(parts, e_refs, o_refs):
    ga = e_refs[0][...].astype(F32)
    gb = e_refs[1][...].astype(F32)
    o_refs[0][...] = (_sigmoid(ga) * parts[0] + _sigmoid(gb) * parts[1]).astype(o_refs[0].dtype)


def _rope128(g, c, s1, s2):
    return g * c + pltpu.roll(g, 32, 1) * s1 + pltpu.roll(g, LANES - 32, 1) * s2


def _ep_mla_pre(parts, e_refs, o_refs, *, q_rank, kv_rank, eps):
    acc = parts[0]
    qg, kvg, c, s1, s2 = (r[...] for r in e_refs)
    cq = acc[:, :q_rank]
    ckv = acc[:, q_rank:q_rank + kv_rank]
    kp = acc[:, q_rank + kv_rank:q_rank + kv_rank + LANES]
    cq = cq * lax.rsqrt(jnp.mean(cq * cq, axis=-1, keepdims=True) + eps) * qg
    ckv = ckv * lax.rsqrt(jnp.mean(ckv * ckv, axis=-1, keepdims=True) + eps) * kvg
    o_refs[0][...] = cq.astype(o_refs[0].dtype)
    o_refs[1][...] = ckv.astype(o_refs[1].dtype)
    o_refs[2][...] = _rope128(kp, c, s1, s2).astype(o_refs[2].dtype)


def _ep_mla_q(parts, e_refs, o_refs, *, heads, scale):
    acc = parts[0]
    c, s1, s2 = (r[...] for r in e_refs)
    for h in range(heads):
        base = 2 * LANES * h
        o_refs[0][h, :, :LANES] = (acc[:, base:base + LANES] * scale).astype(o_refs[0].dtype)
        pe = _rope128(acc[:, base + LANES:base + 2 * LANES], c, s1, s2)
        o_refs[0][h, :, LANES:] = (pe * scale).astype(o_refs[0].dtype)


def _ep_mla_kv(parts, e_refs, o_refs, *, heads):
    acc = parts[0]
    kpe = e_refs[0][...]
    for h in range(heads):
        base = 2 * LANES * h
        o_refs[0][h, :, :LANES] = acc[:, base:base + LANES].astype(o_refs[0].dtype)
        o_refs[0][h, :, LANES:] = kpe
        o_refs[1][h] = acc[:, base + LANES:base + 2 * LANES].astype(o_refs[1].dtype)


ATTN_SPLIT = 2


def _attn_kernel(q_ref, k_ref, v_ref, o_ref, *, tq, chunk):
    qi = pl.program_id(2)
    dv = v_ref.shape[-1]
    rows = tq // ATTN_SPLIT
    qs = [q_ref[pl.ds(i * rows, rows), :] for i in range(ATTN_SPLIT)]
    r_chunk = lax.broadcasted_iota(jnp.int32, (rows, rows), 0) // chunk
    c_chunk = lax.broadcasted_iota(jnp.int32, (rows, rows), 1) // chunk
    diag_mask = c_chunk <= r_chunk

    def update(q, k, v, carry, mask):
        m, l, acc = carry
        s = lax.dot_general(q, k, (((1,), (1,)), ((), ())), preferred_element_type=F32)
        if mask is not None:
            s = jnp.where(mask, s, -jnp.inf)
        m_new = jnp.maximum(m, jnp.max(s, axis=1, keepdims=True))
        p = jnp.exp2(s - m_new[:, :1])
        alpha = jnp.exp2(m - m_new)
        l = alpha * l + jnp.sum(p, axis=1, keepdims=True)
        acc = alpha[:, :dv] * acc + jnp.dot(p.astype(BF16), v, preferred_element_type=F32)
        return m_new, l, acc

    def full_block(kb, carry):
        ks = pl.multiple_of(kb * tq, tq)
        k = k_ref[pl.ds(ks, tq), :]
        v = v_ref[pl.ds(ks, tq), :]
        return tuple(update(qs[i], k, v, carry[i], None) for i in range(ATTN_SPLIT))

    init = tuple((jnp.full((rows, LANES), -jnp.inf, F32), jnp.zeros((rows, LANES), F32), jnp.zeros((rows, dv), F32))
                 for _ in range(ATTN_SPLIT))
    carry = list(lax.fori_loop(0, qi, full_block, init))
    for i in range(ATTN_SPLIT):
        for d in range(i + 1):
            ks = pl.multiple_of(qi * tq + d * rows, rows)
            k = k_ref[pl.ds(ks, rows), :]
            v = v_ref[pl.ds(ks, rows), :]
            carry[i] = update(qs[i], k, v, carry[i], diag_mask if d == i else None)
    for i in range(ATTN_SPLIT):
        _, l, acc = carry[i]
        o_ref[pl.ds(i * rows, rows), :] = (acc / l[:, :dv]).astype(o_ref.dtype)


def attention(q_full, k_full, v, *, tq, chunk):
    b, h, s, dq = q_full.shape
    dv = v.shape[-1]
    assert s % tq == 0 and (tq // ATTN_SPLIT) % chunk == 0 and dv == LANES
    return pl.pallas_call(
        functools.partial(_attn_kernel, tq=tq, chunk=chunk),
        grid=(b, h, s // tq),
        in_specs=[
            pl.BlockSpec((None, None, tq, dq), lambda bi, hi, qi: (bi, hi, qi, 0)),
            pl.BlockSpec((None, None, s, dq), lambda bi, hi, qi: (bi, hi, 0, 0)),
            pl.BlockSpec((None, None, s, dv), lambda bi, hi, qi: (bi, hi, 0, 0)),
        ],
        out_specs=pl.BlockSpec((None, tq, dv), lambda bi, hi, qi: (bi, qi, hi)),
        out_shape=jax.ShapeDtypeStruct((b, s, h * dv), BF16),
        compiler_params=_cparams(("parallel", "parallel", "arbitrary")),
        name="mla_attention",
    )(q_full, k_full, v)


def _split3(x):
    hi = x.astype(BF16)
    r = x - hi.astype(F32)
    mid = r.astype(BF16)
    lo = (r - mid.astype(F32)).astype(BF16)
    return hi, mid, lo


def _hgrn_kernel(q_ref, f_ref, i_ref, og_ref, lb_ref, gain_ref, o_ref, state_ref, *, rows, eps):
    @pl.when(pl.program_id(2) == 0)
    def _():
        state_ref[...] = jnp.zeros_like(state_ref)

    dk = q_ref.shape[-1]
    dv = i_ref.shape[-1]
    lb = lb_ref[...]
    log_lb = jnp.log(lb)
    log_1m_lb = jnp.log1p(-lb)
    one_m_lb = 1.0 - lb
    gain = gain_ref[...]
    cl = HG_CHUNK
    halves = [1 << l for l in range(cl.bit_length() - 1)]
    chunks = [slice(c, c + cl) for c in range(0, rows, cl)]
    ones_k = jnp.ones((dk, LANES), BF16)
    ones_v = jnp.full((dv, LANES), 1.0 / dv, F32).astype(BF16)

    t_id = lax.broadcasted_iota(jnp.int32, (cl, cl), 0)
    s_id = lax.broadcasted_iota(jnp.int32, (cl, cl), 1)
    row = lax.broadcasted_iota(jnp.int32, (cl, dk), 0)
    cum_rows = [s_id <= t_id] + [s_id <= (t_id // (2 * h)) * (2 * h) + h for h in halves]
    cum_rows = jnp.concatenate([jnp.where(m, 1.0, 0.0) for m in cum_rows], axis=0).astype(BF16)
    upper = [(row % (2 * h)) >= h for h in halves]
    pair = [(t_id // (2 * h) == s_id // (2 * h)) & (t_id % (2 * h) >= h) & (s_id % (2 * h) < h) for h in halves]

    z = f_ref[...]
    y = log_1m_lb + (jnp.minimum(z, 0.0) - jnp.log1p(jnp.exp(-jnp.abs(z))))
    g = jnp.maximum(log_lb, y) + jnp.log1p(jnp.exp(-jnp.abs(log_lb - y)))
    kk = one_m_lb / (1.0 + jnp.exp(z))
    q = _silu(q_ref[...])
    v = i_ref[...]
    v16 = v.astype(BF16)
    g3 = jnp.concatenate(_split3(g), axis=1)

    cums = []
    for sl in chunks:
        c3 = jnp.dot(cum_rows, g3[sl], preferred_element_type=F32)
        cums.append(c3[:, :dk] + c3[:, dk:2 * dk] + c3[:, 2 * dk:])

    scores = []
    for sl, cum in zip(chunks, cums):
        b = cum[:cl]
        qc, kc = q[sl], kk[sl]
        acc = None
        for l, h in enumerate(halves):
            e = jnp.exp(-jnp.abs(b - cum[cl * (l + 1):cl * (l + 2)]))
            x = (jnp.where(upper[l], qc, kc) * e).astype(BF16)
            p = lax.dot_general(x, x, (((1,), (1,)), ((), ())), preferred_element_type=F32)
            p = jnp.where(pair[l], p, 0.0)
            acc = p if acc is None else acc + p
        scores.append(acc.astype(BF16))

    incs, decs, q_decs = [], [], []
    for sl, cum in zip(chunks, cums):
        b = cum[:cl]
        b_last = b[cl - 1:cl, :]
        k_dec = kk[sl] * jnp.exp(b_last - b)
        incs.append(lax.dot_general(v16[sl], k_dec.astype(BF16), (((0,), (0,)), ((), ())),
                                    preferred_element_type=F32))
        decs.append(jnp.exp(b_last))
        q_decs.append((q[sl] * jnp.exp(b)).astype(BF16))
    st = state_ref[...]
    o_inter = []
    for j in range(len(chunks)):
        o_inter.append(lax.dot_general(q_decs[j], st.astype(BF16), (((1,), (1,)), ((), ())),
                                       preferred_element_type=F32))
        st = st * decs[j] + incs[j]
    state_ref[...] = st

    outs = []
    for j, sl in enumerate(chunks):
        diag = jnp.dot((q[sl] * kk[sl]).astype(BF16), ones_k, preferred_element_type=F32)[:, :dv]
        outs.append(o_inter[j] + jnp.dot(scores[j], v16[sl], preferred_element_type=F32) + diag * v[sl])
    o = jnp.concatenate(outs, axis=0)
    ms = jnp.dot((o * o).astype(BF16), ones_v, preferred_element_type=F32)[:, :dv]
    o = o * lax.rsqrt(ms + eps) * gain * _silu(og_ref[...])
    o_ref[...] = o.astype(o_ref.dtype)


def hgrn2(hg, lb, gain, *, heads, dk, dv, rows, eps):
    b, s, _ = hg.shape
    assert dk == LANES and dv == LANES and s % rows == 0 and rows % HG_CHUNK == 0

    def col(group):
        return pl.BlockSpec((None, rows, dk), lambda bi, hi, si: (bi, si, group * heads + hi))

    vec = pl.BlockSpec((1, dk), lambda bi, hi, si: (0, hi))
    return pl.pallas_call(
        functools.partial(_hgrn_kernel, rows=rows, eps=eps),
        grid=(b, heads, s // rows),
        in_specs=[col(0), col(1), col(2), col(3), vec, vec],
        out_specs=pl.BlockSpec((None, rows, dv), lambda bi, hi, si: (bi, si, hi)),
        out_shape=jax.ShapeDtypeStruct((b, s, heads * dv), BF16),
        scratch_shapes=[pltpu.VMEM((dv, dk), F32)],
        compiler_params=_cparams(("parallel", "parallel", "arbitrary")),
        name="hgrn2",
    )(hg, hg, hg, hg, lb.reshape(1, -1), gain.reshape(1, -1))


META_E1, META_E2, META_R1, META_R2, META_W1, META_W2 = range(6)
HI16 = 0xFFFF0000


def _pack_bf16_pair(lo, hi):
    lo_bits = lax.bitcast_convert_type(lo.astype(jnp.bfloat16).astype(F32), jnp.uint32) >> 16
    hi_bits = lax.bitcast_convert_type(hi.astype(jnp.bfloat16).astype(F32), jnp.uint32)
    return hi_bits | lo_bits


def _unpack_bf16_pair(word):
    lo = lax.bitcast_convert_type(word << 16, F32).astype(BF16)
    hi = lax.bitcast_convert_type(word & jnp.uint32(HI16), F32).astype(BF16)
    return lo, hi


def _router_kernel(x_ref, g_ref, wr_ref, hp_ref, meta_ref, cnt_ref, base_ref, *, n_experts, eps):
    @pl.when(pl.program_id(0) == 0)
    def _():
        base_ref[...] = jnp.zeros_like(base_ref)

    tm, d = x_ref.shape
    half = d // 2
    x = x_ref[...]
    h = x * lax.rsqrt(jnp.mean(x * x, axis=-1, keepdims=True) + eps) * g_ref[...]
    hp_ref[...] = _pack_bf16_pair(h[:, :half], h[:, half:])

    h_hi = h.astype(BF16)
    h_lo = (h - h_hi.astype(F32)).astype(BF16)
    wr = wr_ref[...]
    w_hi = wr.astype(BF16)
    w_lo = (wr - w_hi.astype(F32)).astype(BF16)
    logits = (jnp.dot(h_hi, w_hi, preferred_element_type=F32)
              + (jnp.dot(h_lo, w_hi, preferred_element_type=F32) + jnp.dot(h_hi, w_lo, preferred_element_type=F32)))
    lane = lax.broadcasted_iota(jnp.int32, logits.shape, 1)
    logits = jnp.where(lane < n_experts, logits, -jnp.inf)
    m1 = jnp.max(logits, axis=1, keepdims=True)
    i1 = jnp.min(jnp.where(logits == m1, lane, LANES), axis=1, keepdims=True)
    rest = jnp.where(lane == i1, -jnp.inf, logits)
    m2 = jnp.max(rest, axis=1, keepdims=True)
    i2 = jnp.min(jnp.where(rest == m2, lane, LANES), axis=1, keepdims=True)
    e2 = jnp.exp(m2 - m1)
    w1 = 1.0 / (1.0 + e2)
    w2 = e2 / (1.0 + e2)

    sel1 = lane == i1
    sel2 = lane == i2
    sel = jnp.where(sel1 | sel2, 1.0, 0.0)
    r_id = lax.broadcasted_iota(jnp.int32, (tm, tm), 0)
    c_id = lax.broadcasted_iota(jnp.int32, (tm, tm), 1)
    before = jnp.where(c_id < r_id, 1.0, 0.0).astype(BF16)
    rank = base_ref[...] + jnp.dot(before, sel.astype(BF16), preferred_element_type=F32)
    r1 = jnp.sum(jnp.where(sel1, rank, 0.0), axis=1, keepdims=True)
    r2 = jnp.sum(jnp.where(sel2, rank, 0.0), axis=1, keepdims=True)
    base_ref[...] += jnp.sum(sel, axis=0, keepdims=True)
    cnt_ref[...] = jnp.broadcast_to(base_ref[...], cnt_ref.shape)

    record = jnp.zeros(logits.shape, F32)
    for lane_id, val in ((META_E1, i1.astype(F32)), (META_E2, i2.astype(F32)), (META_R1, r1), (META_R2, r2),
                         (META_W1, w1), (META_W2, w2)):
        record = jnp.where(lane == lane_id, val, record)
    meta_ref[...] = record


def router(x, g, w_router, *, tm, eps):
    n, d = x.shape
    n_experts = w_router.shape[1]
    wr = jnp.zeros((d, LANES), F32).at[:, :n_experts].set(w_router)
    return pl.pallas_call(
        functools.partial(_router_kernel, n_experts=n_experts, eps=eps),
        grid=(n // tm,),
        in_specs=[pl.BlockSpec((tm, d), lambda i: (i, 0)), pl.BlockSpec((1, d), lambda i: (0, 0)),
                  pl.BlockSpec((d, LANES), lambda i: (0, 0))],
        out_specs=[pl.BlockSpec((tm, d // 2), lambda i: (i, 0)), pl.BlockSpec((tm, LANES), lambda i: (i, 0)),
                   pl.BlockSpec((8, LANES), lambda i: (0, 0))],
        out_shape=[jax.ShapeDtypeStruct((n, d // 2), jnp.uint32), jax.ShapeDtypeStruct((n, LANES), F32),
                   jax.ShapeDtypeStruct((8, LANES), F32)],
        scratch_shapes=[pltpu.VMEM((1, LANES), F32)],
        compiler_params=_cparams(("arbitrary",)),
        name="ffn_norm_router",
    )(x, g.reshape(1, d), wr)


def _row_copy(src_ref, src_row, dst_ref, dst_row, sem):
    return pltpu.make_async_copy(src_ref.at[pl.ds(src_row, 1), :], dst_ref.at[pl.ds(dst_row, 1), :], sem)


def _dispatch_kernel(dest_ref, h_ref, zeros_ref, xs_ref, sem, *, tokens):
    del zeros_ref
    base = pl.program_id(0) * tokens

    def issue(t, carry):
        for s in range(2):
            _row_copy(h_ref, t, xs_ref, dest_ref[2 * (base + t) + s], sem).start()
        return carry

    def drain(t, carry):
        for s in range(2):
            _row_copy(h_ref, 0, xs_ref, 0, sem).wait()
        return carry

    lax.fori_loop(0, tokens, issue, 0)
    lax.fori_loop(0, tokens, drain, 0)


def dispatch(dest, hp, rows_padded, *, tokens):
    n, words = hp.shape
    zeros = jnp.zeros((rows_padded, words), hp.dtype)
    return pl.pallas_call(
        functools.partial(_dispatch_kernel, tokens=tokens),
        grid_spec=pltpu.PrefetchScalarGridSpec(
            num_scalar_prefetch=1, grid=(n // tokens,),
            in_specs=[pl.BlockSpec((tokens, words), lambda i, dest: (i, 0)),
                      pl.BlockSpec(memory_space=pl.ANY)],
            out_specs=pl.BlockSpec(memory_space=pl.ANY),
            scratch_shapes=[pltpu.SemaphoreType.DMA(())]),
        out_shape=jax.ShapeDtypeStruct(zeros.shape, zeros.dtype),
        input_output_aliases={2: 0},
        compiler_params=_cparams(("arbitrary",)),
        name="moe_dispatch",
    )(dest, hp, zeros)


def _expert_up_kernel(te_ref, nu_ref, x_ref, w1_ref, w3_ref, u_ref, xb_ref):
    i, j = pl.program_id(0), pl.program_id(1)
    used = i < nu_ref[0]
    half = x_ref.shape[1]

    @pl.when(used & (j == 0))
    def _():
        lo, hi = _unpack_bf16_pair(x_ref[...])
        xb_ref[:, :half] = lo
        xb_ref[:, half:] = hi

    @pl.when(used)
    def _():
        xb = xb_ref[...]
        a1 = jnp.dot(xb, w1_ref[...], preferred_element_type=F32)
        a3 = jnp.dot(xb, w3_ref[...], preferred_element_type=F32)
        u_ref[...] = (_silu(a1) * a3).astype(u_ref.dtype)

    @pl.when(jnp.logical_not(used))
    def _():
        u_ref[...] = jnp.zeros_like(u_ref)


def _expert_down_kernel(te_ref, nu_ref, u_ref, w2_ref, y_ref):
    used = pl.program_id(0) < nu_ref[0]
    half = w2_ref.shape[1] // 2

    @pl.when(used)
    def _():
        acc = jnp.dot(u_ref[...], w2_ref[...], preferred_element_type=F32)
        y_ref[...] = _pack_bf16_pair(acc[:, :half], acc[:, half:])

    @pl.when(jnp.logical_not(used))
    def _():
        y_ref[...] = jnp.zeros_like(y_ref)


def expert_ffn(tile_expert, n_used, xs, w1, w3, w2, *, tm, tn):
    n_exp, d, dff = w1.shape
    rows = xs.shape[0]
    n_tiles = rows // tm

    def tile(i, nu):
        return jnp.minimum(i, nu[0] - 1)

    def col(i, j, nu, n_col):
        return jnp.where(i < nu[0], j, n_col - 1)

    u = pl.pallas_call(
        _expert_up_kernel,
        grid_spec=pltpu.PrefetchScalarGridSpec(
            num_scalar_prefetch=2, grid=(n_tiles, dff // tn),
            in_specs=[pl.BlockSpec((tm, d // 2), lambda i, j, te, nu: (tile(i, nu), 0)),
                      pl.BlockSpec((None, d, tn), lambda i, j, te, nu: (te[i], 0, col(i, j, nu, dff // tn))),
                      pl.BlockSpec((None, d, tn), lambda i, j, te, nu: (te[i], 0, col(i, j, nu, dff // tn)))],
            out_specs=pl.BlockSpec((tm, tn), lambda i, j, te, nu: (i, j)),
            scratch_shapes=[pltpu.VMEM((tm, d), BF16)]),
        out_shape=jax.ShapeDtypeStruct((rows, dff), BF16),
        compiler_params=_cparams(("arbitrary", "arbitrary")),
        name="moe_expert_up",
    )(tile_expert, n_used, xs, w1, w3)
    return pl.pallas_call(
        _expert_down_kernel,
        grid_spec=pltpu.PrefetchScalarGridSpec(
            num_scalar_prefetch=2, grid=(n_tiles, d // tn),
            in_specs=[pl.BlockSpec((tm, dff), lambda i, j, te, nu: (tile(i, nu), 0)),
                      pl.BlockSpec((None, dff, tn), lambda i, j, te, nu: (te[i], 0, col(i, j, nu, d // tn)))],
            out_specs=pl.BlockSpec((tm, tn // 2), lambda i, j, te, nu: (i, j))),
        out_shape=jax.ShapeDtypeStruct((rows, d // 2), jnp.uint32),
        compiler_params=_cparams(("arbitrary", "arbitrary")),
        name="moe_expert_down",
    )(tile_expert, n_used, u, w2)


def _combine_kernel(dest_ref, x_ref, meta_ref, g_ref, y_ref, o_ref, buf_ref, sem, *, tokens, pair_block, eps,
                    final_norm):
    i = pl.program_id(0)
    slot = i % 2
    words = buf_ref.shape[-1]

    def fetch(step, into):
        def issue(t, carry):
            for s in range(2):
                _row_copy(y_ref, dest_ref[2 * (step * tokens + t) + s], buf_ref.at[into, s], t, sem.at[into]).start()
            return carry

        lax.fori_loop(0, tokens, issue, 0)

    @pl.when(i == 0)
    def _():
        fetch(0, 0)

    @pl.when(i + 1 < pl.num_programs(0))
    def _():
        fetch(i + 1, 1 - slot)

    def drain(t, carry):
        for s in range(2):
            _row_copy(y_ref, 0, buf_ref.at[slot, s], 0, sem.at[slot]).wait()
        return carry

    lax.fori_loop(0, tokens, drain, 0)

    meta = meta_ref[...]
    w1 = meta[:, META_W1:META_W1 + 1]
    w2 = meta[:, META_W2:META_W2 + 1]
    w1 = jnp.broadcast_to(w1, (tokens, LANES))
    w2 = jnp.broadcast_to(w2, (tokens, LANES))
    sq = jnp.zeros((tokens, LANES), F32)
    half = pair_block // 2
    chunks = half // LANES
    for c in range(words // LANES):
        lo_col = (c // chunks) * pair_block + (c % chunks) * LANES
        y1 = buf_ref[slot, 0, :, c * LANES:(c + 1) * LANES]
        y2 = buf_ref[slot, 1, :, c * LANES:(c + 1) * LANES]
        for col, shift in ((lo_col, True), (lo_col + half, False)):
            cols = slice(col, col + LANES)
            a1 = lax.bitcast_convert_type(y1 << 16 if shift else y1 & jnp.uint32(HI16), F32)
            a2 = lax.bitcast_convert_type(y2 << 16 if shift else y2 & jnp.uint32(HI16), F32)
            o = x_ref[:, cols] + w1 * a1 + w2 * a2
            o_ref[:, cols] = o
            sq = sq + o * o
    if final_norm:
        d = 2 * words
        inv = jnp.broadcast_to(lax.rsqrt(jnp.sum(sq, axis=1, keepdims=True) / d + eps), (tokens, LANES))
        for c in range(d // LANES):
            cols = slice(c * LANES, (c + 1) * LANES)
            o_ref[:, cols] = o_ref[:, cols] * inv * g_ref[:, cols]


def combine(dest, x, meta, gain, y, *, tokens, pair_block, eps, final_norm):
    n, d = x.shape
    return pl.pallas_call(
        functools.partial(_combine_kernel, tokens=tokens, pair_block=pair_block, eps=eps, final_norm=final_norm),
        grid_spec=pltpu.PrefetchScalarGridSpec(
            num_scalar_prefetch=1, grid=(n // tokens,),
            in_specs=[pl.BlockSpec((tokens, d), lambda i, dest: (i, 0)),
                      pl.BlockSpec((tokens, LANES), lambda i, dest: (i, 0)),
                      pl.BlockSpec((1, d), lambda i, dest: (0, 0)),
                      pl.BlockSpec(memory_space=pl.ANY)],
            out_specs=pl.BlockSpec((tokens, d), lambda i, dest: (i, 0)),
            scratch_shapes=[pltpu.VMEM((2, 2, tokens, d // 2), jnp.uint32), pltpu.SemaphoreType.DMA((2,))]),
        out_shape=jax.ShapeDtypeStruct((n, d), F32),
        compiler_params=_cparams(("arbitrary",)),
        name="moe_combine",
    )(dest, x, meta, gain.reshape(1, d), y)


def _rope_tables(positions, dm):
    half = dm.mla_rope // 2
    inv_freq = dm.rope_theta ** (-jnp.arange(half, dtype=F32) / half)
    ang = positions.astype(F32).reshape(-1, 1) * inv_freq
    cos, sin = jnp.cos(ang), jnp.sin(ang)
    zero = jnp.zeros_like(cos)
    c = jnp.concatenate([cos, cos, zero, zero], axis=-1)
    s1 = jnp.concatenate([zero, sin, zero, zero], axis=-1)
    s2 = jnp.concatenate([-sin, zero, zero, zero], axis=-1)
    return c, s1, s2


def _mixer(x, lp, tables, dm):
    n = dm.batch * dm.seq
    d = dm.d_model
    hg_cols = 4 * dm.hg_heads * dm.hg_dk
    c, s1, s2 = tables
    h = rmsnorm(x, lp["norm_mix"], BF16, tm=dm.tm_norm, eps=dm.eps)

    (hg,) = matmul([h], [lp["w_hg"]], [0], [], [], [jax.ShapeDtypeStruct((n, hg_cols), F32)],
                   [_tile_spec(dm.tm, dm.tn)], _ep_store, tm=dm.tm, tn=dm.tn, name="proj_hgrn")
    (gates,) = matmul([h], [lp["w_gates"]], [0], [], [], [jax.ShapeDtypeStruct((n, 2 * d), BF16)],
                      [_tile_spec(dm.tm, dm.tn)], _ep_store, tm=dm.tm, tn=dm.tn, name="proj_gates")

    mla_cols = lp["w_mla"].shape[1]
    tms = dm.tm_small
    cq, ckv, kpe = matmul(
        [h], [lp["w_mla"]], [0],
        [lp["q_gain"], lp["kv_gain"], c, s1, s2],
        [pl.BlockSpec((1, dm.mla_q_rank), lambda i, j, k: (0, 0)),
         pl.BlockSpec((1, dm.mla_kv_rank), lambda i, j, k: (0, 0)),
         _row_spec(tms, LANES), _row_spec(tms, LANES), _row_spec(tms, LANES)],
        [jax.ShapeDtypeStruct((n, dm.mla_q_rank), BF16), jax.ShapeDtypeStruct((n, dm.mla_kv_rank), BF16),
         jax.ShapeDtypeStruct((n, LANES), BF16)],
        [_row_spec(tms, dm.mla_q_rank), _row_spec(tms, dm.mla_kv_rank), _row_spec(tms, LANES)],
        functools.partial(_ep_mla_pre, q_rank=dm.mla_q_rank, kv_rank=dm.mla_kv_rank, eps=dm.eps),
        tm=tms, tn=mla_cols, name="proj_mla_latents")

    heads = dm.mla_heads
    hb = min(4, heads)
    tmu = dm.tm_up
    s_tiles = dm.seq // tmu

    def head_spec(width):
        return pl.BlockSpec((None, hb, tmu, width), lambda i, j, k: (i // s_tiles, j, i % s_tiles, 0))

    scale = (dm.mla_nope + dm.mla_rope) ** -0.5 * LOG2_E
    (q_full,) = matmul(
        [cq], [lp["w_uq"]], [0], [c, s1, s2],
        [_row_spec(tmu, LANES), _row_spec(tmu, LANES), _row_spec(tmu, LANES)],
        [jax.ShapeDtypeStruct((dm.batch, heads, dm.seq, 2 * LANES), BF16)], [head_spec(2 * LANES)],
        functools.partial(_ep_mla_q, heads=hb, scale=scale), tm=tmu, tn=hb * 2 * LANES, name="mla_q_up")
    k_full, v = matmul(
        [ckv], [lp["w_ukv"]], [0], [kpe], [_row_spec(tmu, LANES)],
        [jax.ShapeDtypeStruct((dm.batch, heads, dm.seq, 2 * LANES), BF16),
         jax.ShapeDtypeStruct((dm.batch, heads, dm.seq, dm.mla_dv), BF16)],
        [head_spec(2 * LANES), head_spec(dm.mla_dv)],
        functools.partial(_ep_mla_kv, heads=hb), tm=tmu, tn=hb * 2 * LANES, name="mla_kv_up")
    o_b = attention(q_full, k_full, v, tq=dm.tq, chunk=dm.chunk).reshape(n, heads * dm.mla_dv)

    o_a = hgrn2(hg.reshape(dm.batch, dm.seq, hg_cols), lp["lb"], lp["hg_gain"], heads=dm.hg_heads, dk=dm.hg_dk,
                dv=dm.hg_dv, rows=dm.hgrn_rows, eps=dm.eps).reshape(n, dm.hg_heads * dm.hg_dv)

    (y,) = matmul([o_a, o_b], [lp["w_branch_a"], lp["w_branch_b"]], [0, 1], [gates, gates],
                  [_tile_spec(dm.tm, dm.tn2), _tile_spec(dm.tm, dm.tn2, d // dm.tn2)],
                  [jax.ShapeDtypeStruct((n, d), BF16)], [_tile_spec(dm.tm, dm.tn2)], _ep_gated_sum,
                  tm=dm.tm, tn=dm.tn2, name="branch_merge")
    (x,) = matmul([y], [lp["w_out"]], [0], [x], [_tile_spec(dm.tm, dm.tn2)],
                  [jax.ShapeDtypeStruct((n, d), F32)], [_tile_spec(dm.tm, dm.tn2)], _ep_residual,
                  tm=dm.tm, tn=dm.tn2, name="mixer_out")
    return x


def _dense_ffn(x, lp, dm):
    n, d = x.shape
    h = rmsnorm(x, lp["norm_ffn"], BF16, tm=dm.tm_norm, eps=dm.eps)
    (u,) = matmul([h], [lp["w1"], lp["w3"]], [0, 0], [], [], [jax.ShapeDtypeStruct((n, dm.d_ff_pad), BF16)],
                  [_tile_spec(dm.tm, dm.tn2)], _ep_swiglu, tm=dm.tm, tn=dm.tn2, name="ffn_up")
    (x,) = matmul([u], [lp["w2"]], [0], [x], [_tile_spec(dm.tm, dm.tn)], [jax.ShapeDtypeStruct((n, d), F32)],
                  [_tile_spec(dm.tm, dm.tn)], _ep_residual, tm=dm.tm, tn=dm.tn, tk=dm.tk_ffn, name="ffn_down")
    return x


def _moe_ffn(x, lp, dm, final_gain):
    n, d = x.shape
    n_exp, tme = dm.n_experts, dm.tm_expert
    hp, meta, cnt = router(x, lp["norm_ffn"], lp["w_router"], tm=dm.tm_norm, eps=dm.eps)
    counts = cnt[0, :n_exp].astype(jnp.int32)
    padded = (counts + tme - 1) // tme * tme
    ends = jnp.cumsum(padded)
    starts = ends - padded
    e12 = meta[:, META_E1:META_E2 + 1].astype(jnp.int32)
    r12 = meta[:, META_R1:META_R2 + 1].astype(jnp.int32)
    dest = (starts[e12] + r12).reshape(-1)
    n_tiles = (2 * n) // tme + n_exp
    n_used = (ends[-1] // tme).reshape(1)
    tile_row = jnp.minimum(jnp.arange(n_tiles, dtype=jnp.int32), n_used - 1) * tme
    tile_expert = jnp.minimum(jnp.sum(tile_row[:, None] >= ends[None, :], axis=1), n_exp - 1).astype(jnp.int32)

    xs = dispatch(dest, hp, n_tiles * tme, tokens=dm.tm_small)
    y = expert_ffn(tile_expert, n_used, xs, lp["w1"], lp["w3"], lp["w2"], tm=tme, tn=dm.tn2)
    gain = jnp.ones((d,), F32) if final_gain is None else final_gain
    return combine(dest, x, meta, gain, y, tokens=dm.tm_combine, pair_block=dm.tn2, eps=dm.eps,
                   final_norm=final_gain is not None)


def _prepare_layer(l, p, lbs, dm):
    d = dm.d_model
    hg_cols = 4 * dm.hg_heads * dm.hg_dk
    mla_in = dm.mla_q_rank + dm.mla_kv_rank + dm.mla_rope
    w_in = p["w_in"][l]
    w_mla = jnp.pad(w_in[:, hg_cols:hg_cols + mla_in], ((0, 0), (0, LANES - dm.mla_rope)))
    heads = dm.mla_heads
    w_uq = p["w_uq"][l].reshape(dm.mla_q_rank, heads, dm.mla_nope + dm.mla_rope)
    w_uq = jnp.pad(w_uq, ((0, 0), (0, 0), (0, 2 * LANES - dm.mla_nope - dm.mla_rope)))
    hg_width = dm.hg_heads * dm.hg_dv
    lp = {
        "norm_mix": p["norm_mix"][l],
        "w_hg": w_in[:, :hg_cols].astype(BF16),
        "w_mla": w_mla.astype(BF16),
        "w_gates": w_in[:, hg_cols + mla_in:].astype(BF16),
        "lb": lbs[l],
        "hg_gain": p["hg_norm"][l],
        "q_gain": p["mla_q_norm"][l].reshape(1, -1),
        "kv_gain": p["mla_kv_norm"][l].reshape(1, -1),
        "w_uq": w_uq.reshape(dm.mla_q_rank, heads * 2 * LANES).astype(BF16),
        "w_ukv": p["w_ukv"][l].astype(BF16),
        "w_branch_a": p["w_branch"][l, :hg_width].astype(BF16),
        "w_branch_b": p["w_branch"][l, hg_width:].astype(BF16),
        "w_out": p["w_out"][l].astype(BF16),
        "norm_ffn": p["norm_ffn"][l],
    }
    if l % 2 == 0:
        pad = dm.d_ff_pad - dm.d_ff
        lp["w1"] = jnp.pad(p["ffn_w1"][l // 2], ((0, 0), (0, pad))).astype(BF16)
        lp["w3"] = jnp.pad(p["ffn_w3"][l // 2], ((0, 0), (0, pad))).astype(BF16)
        lp["w2"] = jnp.pad(p["ffn_w2"][l // 2], ((0, pad), (0, 0))).astype(BF16)
    else:
        lp["w_router"] = p["w_router"][l // 2]
        lp["w1"] = p["moe_w1"][l // 2].astype(BF16)
        lp["w3"] = p["moe_w3"][l // 2].astype(BF16)
        lp["w2"] = p["moe_w2"][l // 2].astype(BF16)
    return lp


def forward(p, dm):
    n = dm.batch * dm.seq
    x = p["x"].reshape(n, dm.d_model)
    tables = _rope_tables(p["positions"], dm)
    lbs = jnp.cumsum(jax.nn.softmax(p["hg_lb_logits"].astype(F32), axis=0), axis=0)
    lbs = lbs - lbs[0:1]
    for l in range(dm.depth):
        lp = _prepare_layer(l, p, lbs, dm)
        x = _mixer(x, lp, tables, dm)
        last = l == dm.depth - 1
        if l % 2 == 0:
            x = _dense_ffn(x, lp, dm)
            if last:
                x = rmsnorm(x, p["norm_final"], F32, tm=dm.tm_norm, eps=dm.eps)
        else:
            x = _moe_ffn(x, lp, dm, p["norm_final"] if last else None)
    return x.reshape(dm.batch, dm.seq, dm.d_model)


def kernel(x, positions, norm_mix, w_in, hg_lb_logits, hg_norm, mla_q_norm, w_uq, mla_kv_norm, w_ukv, w_branch, w_out, norm_ffn, ffn_w1, ffn_w3, ffn_w2, w_router, moe_w1, moe_w3, moe_w2, norm_final):
    p = dict(x=x, positions=positions, norm_mix=norm_mix, w_in=w_in, hg_lb_logits=hg_lb_logits, hg_norm=hg_norm,
             mla_q_norm=mla_q_norm, w_uq=w_uq, mla_kv_norm=mla_kv_norm, w_ukv=w_ukv, w_branch=w_branch, w_out=w_out,
             norm_ffn=norm_ffn, ffn_w1=ffn_w1, ffn_w3=ffn_w3, ffn_w2=ffn_w2, w_router=w_router, moe_w1=moe_w1,
             moe_w3=moe_w3, moe_w2=moe_w2, norm_final=norm_final)
    return forward(p, Dims())
```
